```python
import jax, jax.numpy as jnp
from jax import lax
import numpy as np

D_MODEL = 2048
BATCH = 1
SEQ = 8192
DEPTH = 1
DEC_BATCH = 32
DEC_SEQ = 1
PAST_LEN = 8192
PAGE_SIZE = 128

N_HEADS = 16
HEAD_DIM = D_MODEL // N_HEADS
ATTN_WIDTH = N_HEADS * HEAD_DIM
ROT_DIM = HEAD_DIM // 4
ROPE_THETA = 500000.0
MOBA_BLOCK = 256
MOBA_TOPK = 3
Q_CHUNK = 32
POOL_WIDTH = D_MODEL // 2
POOL_WINDOWS = (2, 4, 8, 16)
N_POOL_GROUPS = len(POOL_WINDOWS)
POOL_GROUP = POOL_WIDTH // N_POOL_GROUPS
POOL_BUF = max(POOL_WINDOWS) - 1
D_FF = 256 * ((8 * D_MODEL // 3 + 255) // 256)
IN_COLS = 3 * ATTN_WIDTH + POOL_WIDTH + 2 * D_MODEL
EPS = 1e-6

kernel_name = "moba_pool_macaron_hybrid_step"


def rmsnorm(x, g):
    xf = x.astype(jnp.float32)
    y = xf * lax.rsqrt(jnp.mean(xf * xf, axis=-1, keepdims=True) + EPS)
    return (y * g.astype(jnp.float32)).astype(x.dtype)


def half_ffn(x, g, w_gate, w_up, w_down):
    h = rmsnorm(x, g)
    return x + 0.5 * ((jax.nn.silu(h @ w_gate) * (h @ w_up)) @ w_down)


def rotary(x, pos):
    inv = jnp.power(ROPE_THETA, -jnp.arange(0, ROT_DIM, 2, dtype=jnp.float32) / ROT_DIM)
    ang = pos.astype(jnp.float32)[:, None] * inv[None, :]
    cos = jnp.cos(ang)[None, :, None, :]
    sin = jnp.sin(ang)[None, :, None, :]
    xr = x[..., :ROT_DIM].astype(jnp.float32)
    x1, x2 = xr[..., :ROT_DIM // 2], xr[..., ROT_DIM // 2:]
    rot = jnp.concatenate([x1 * cos - x2 * sin, x2 * cos + x1 * sin], axis=-1)
    return jnp.concatenate([rot.astype(x.dtype), x[..., ROT_DIM:]], axis=-1)


def mixer_inputs(h, pos, w_in):
    B, L, _ = h.shape
    z = h @ w_in
    cuts = [ATTN_WIDTH, 2 * ATTN_WIDTH, 3 * ATTN_WIDTH, 3 * ATTN_WIDTH + POOL_WIDTH,
            3 * ATTN_WIDTH + POOL_WIDTH + D_MODEL]
    q, k, v, u, ga, gb = jnp.split(z, cuts, axis=-1)
    q = rotary(q.reshape(B, L, N_HEADS, HEAD_DIM), pos)
    k = rotary(k.reshape(B, L, N_HEADS, HEAD_DIM), pos)
    v = v.reshape(B, L, N_HEADS, HEAD_DIM)
    return q, k, v, u, ga, gb


def to_blocks(k):
    B, T = k.shape[:2]
    nb = -(-T // MOBA_BLOCK)
    k = jnp.pad(k, ((0, 0), (0, nb * MOBA_BLOCK - T), (0, 0), (0, 0)))
    return k.reshape(B, nb, MOBA_BLOCK, N_HEADS, HEAD_DIM)


def moba_attend(q, q_pos, kb, vb, kmean):
    B, Q = q.shape[:2]
    nb = kb.shape[1]
    n_sel = min(MOBA_TOPK, nb)
    own = q_pos // MOBA_BLOCK
    gate = jnp.einsum('bqhd,bnhd->bqhn', q.astype(jnp.float32), kmean)
    fully_past = jnp.arange(nb)[None, :] < own[:, None]
    gate = jnp.where(fully_past[None, :, None, :], gate, -jnp.inf)
    top_s, top_i = lax.top_k(gate, n_sel)
    own_b = jnp.broadcast_to(own[None, :, None, None], (B, Q, N_HEADS, 1)).astype(top_i.dtype)
    blk = jnp.concatenate([top_i, own_b], axis=-1)
    valid = jnp.concatenate([jnp.isfinite(top_s), jnp.ones((B, Q, N_HEADS, 1), bool)], axis=-1)
    b_ix = jnp.arange(B)[:, None, None, None]
    h_ix = jnp.arange(N_HEADS)[None, None, :, None]
    kg = kb[b_ix, blk, :, h_ix, :]
    vg = vb[b_ix, blk, :, h_ix, :]
    s = jnp.einsum('bqhd,bqhjkd->bqhjk', q, kg).astype(jnp.float32) * (HEAD_DIM ** -0.5)
    key_pos = blk[..., None] * MOBA_BLOCK + jnp.arange(MOBA_BLOCK)
    mask = valid[..., None] & (key_pos <= q_pos[None, :, None, None, None])
    s = jnp.where(mask, s, -jnp.inf).reshape(B, Q, N_HEADS, -1)
    p = jax.nn.softmax(s, axis=-1).astype(vg.dtype)
    return jnp.einsum('bqhn,bqhnd->bqhd', p, vg.reshape(B, Q, N_HEADS, -1, HEAD_DIM))


def pool_mix(u, prefix, pos, pool_map, pool_scale):
    B, L, _ = u.shape
    ext = jnp.concatenate([prefix.astype(u.dtype), u], axis=1)
    extf = ext.astype(jnp.float32)
    cs = jnp.pad(jnp.cumsum(extf, axis=1), ((0, 0), (1, 0), (0, 0)))
    hi = cs[:, POOL_BUF + 1:]
    means = []
    for g, w in enumerate(POOL_WINDOWS):
        sl = slice(g * POOL_GROUP, (g + 1) * POOL_GROUP)
        lo = cs[:, POOL_BUF + 1 - w:POOL_BUF + 1 - w + L, sl]
        cnt = jnp.minimum(w, pos + 1).astype(jnp.float32)[None, :, None]
        means.append((hi[..., sl] - lo) / cnt)
    pooled = jnp.concatenate(means, axis=-1) - extf[:, POOL_BUF:]
    pooled = pooled.reshape(B, L, N_POOL_GROUPS, POOL_GROUP).astype(pool_map.dtype)
    mixed = jnp.einsum('blgc,gcd->blgd', pooled, pool_map).reshape(B, L, POOL_WIDTH)
    return mixed * pool_scale, ext[:, -POOL_BUF:]


def merge(att, pooled, ga, gb, w_branch_attn, w_branch_pool, w_out):
    a = att @ w_branch_attn
    b = pooled @ w_branch_pool
    return (jax.nn.sigmoid(ga) * a + jax.nn.sigmoid(gb) * b) @ w_out


def setup_inputs(seed: int = 0) -> dict:
    key = jax.random.key(seed)
    ks = jax.random.split(key, 24)
    f32 = jnp.float32
    n_pages = PAST_LEN // PAGE_SIZE
    n_used = DEC_BATCH * n_pages
    n_phys = n_used + max(1, n_used // 4)

    def w(k, shape, fan_in):
        return jax.random.normal(k, shape, f32) * (fan_in ** -0.5)

    def gain(k, shape):
        return 1.0 + 0.02 * jax.random.normal(k, shape, f32)

    perm = jax.random.permutation(ks[5], n_phys)[:n_used]
    page_table = perm.reshape(DEC_BATCH, n_pages).astype(jnp.int32)
    return {
        "x_prompt": jax.random.normal(ks[0], (BATCH, SEQ, D_MODEL), f32),
        "x_sample": jax.random.normal(ks[1], (DEC_BATCH, DEC_SEQ, D_MODEL), f32),
        "cache_k": jax.random.normal(ks[2], (DEPTH, n_phys, PAGE_SIZE, N_HEADS, HEAD_DIM), f32),
        "cache_v": jax.random.normal(ks[3], (DEPTH, n_phys, PAGE_SIZE, N_HEADS, HEAD_DIM), f32),
        "page_table": page_table,
        "state_pool": jax.random.normal(ks[4], (DEPTH, DEC_BATCH, POOL_BUF, POOL_WIDTH), f32),
        "norm_ffn1": gain(ks[6], (DEPTH, D_MODEL)),
        "ffn1_gate": w(ks[7], (DEPTH, D_MODEL, D_FF), D_MODEL),
        "ffn1_up": w(ks[8], (DEPTH, D_MODEL, D_FF), D_MODEL),
        "ffn1_down": w(ks[9], (DEPTH, D_FF, D_MODEL), D_FF),
        "norm_mix": gain(ks[10], (DEPTH, D_MODEL)),
        "w_in": w(ks[11], (DEPTH, D_MODEL, IN_COLS), D_MODEL),
        "pool_map": w(ks[12], (DEPTH, N_POOL_GROUPS, POOL_GROUP, POOL_GROUP), POOL_GROUP),
        "pool_scale": gain(ks[13], (DEPTH, POOL_WIDTH)),
        "w_branch_attn": w(ks[14], (DEPTH, ATTN_WIDTH, D_MODEL), ATTN_WIDTH),
        "w_branch_pool": w(ks[15], (DEPTH, POOL_WIDTH, D_MODEL), POOL_WIDTH),
        "w_out": w(ks[16], (DEPTH, D_MODEL, D_MODEL), D_MODEL),
        "norm_ffn2": gain(ks[17], (DEPTH, D_MODEL)),
        "ffn2_gate": w(ks[18], (DEPTH, D_MODEL, D_FF), D_MODEL),
        "ffn2_up": w(ks[19], (DEPTH, D_MODEL, D_FF), D_MODEL),
        "ffn2_down": w(ks[20], (DEPTH, D_FF, D_MODEL), D_FF),
        "norm_final": gain(ks[21], (D_MODEL,)),
    }


def reference(x_prompt, x_sample, cache_k, cache_v, page_table, state_pool,
              norm_ffn1, ffn1_gate, ffn1_up, ffn1_down, norm_mix, w_in, pool_map, pool_scale,
              w_branch_attn, w_branch_pool, w_out, norm_ffn2, ffn2_gate, ffn2_up, ffn2_down,
              norm_final):
    pos_p = jnp.arange(SEQ, dtype=jnp.int32)
    pos_s = PAST_LEN + jnp.arange(DEC_SEQ, dtype=jnp.int32)
    n_chunks = SEQ // Q_CHUNK
    xp, xs = x_prompt, x_sample
    k_p, v_p, k_s, v_s, pl_p, pl_s = [], [], [], [], [], []
    for l in range(DEPTH):
        ffn1 = (norm_ffn1[l], ffn1_gate[l], ffn1_up[l], ffn1_down[l])
        ffn2 = (norm_ffn2[l], ffn2_gate[l], ffn2_up[l], ffn2_down[l])
        mix_w = (w_branch_attn[l], w_branch_pool[l], w_out[l])
        xp = half_ffn(xp, *ffn1)
        xs = half_ffn(xs, *ffn1)

        q, k, v, u, ga, gb = mixer_inputs(rmsnorm(xp, norm_mix[l]), pos_p, w_in[l])
        kb, vb = to_blocks(k), to_blocks(v)
        kmean = kb.astype(jnp.float32).mean(axis=2)
        qc = q.reshape(BATCH, n_chunks, Q_CHUNK, N_HEADS, HEAD_DIM).swapaxes(0, 1)
        att = lax.map(lambda a: moba_attend(a[0], a[1], kb, vb, kmean),
                      (qc, pos_p.reshape(n_chunks, Q_CHUNK)))
        att = att.swapaxes(0, 1).reshape(BATCH, SEQ, ATTN_WIDTH)
        pooled, tail = pool_mix(u, jnp.zeros((BATCH, POOL_BUF, POOL_WIDTH), u.dtype), pos_p,
                                pool_map[l], pool_scale[l])
        xp = xp + merge(att, pooled, ga, gb, *mix_w)
        k_p.append(k)
        v_p.append(v)
        pl_p.append(tail)

        q, k, v, u, ga, gb = mixer_inputs(rmsnorm(xs, norm_mix[l]), pos_s, w_in[l])
        k_past = cache_k[l][page_table].reshape(DEC_BATCH, PAST_LEN, N_HEADS, HEAD_DIM)
        v_past = cache_v[l][page_table].reshape(DEC_BATCH, PAST_LEN, N_HEADS, HEAD_DIM)
        kb = to_blocks(jnp.concatenate([k_past.astype(k.dtype), k], axis=1))
        vb = to_blocks(jnp.concatenate([v_past.astype(v.dtype), v], axis=1))
        kmean = kb.astype(jnp.float32).mean(axis=2)
        att = moba_attend(q, pos_s, kb, vb, kmean).reshape(DEC_BATCH, DEC_SEQ, ATTN_WIDTH)
        pooled, tail = pool_mix(u, state_pool[l], pos_s, pool_map[l], pool_scale[l])
        xs = xs + merge(att, pooled, ga, gb, *mix_w)
        k_s.append(k)
        v_s.append(v)
        pl_s.append(tail)

        xp = half_ffn(xp, *ffn2)
        xs = half_ffn(xs, *ffn2)
    y_prompt = rmsnorm(xp, norm_final)
    y_sample = rmsnorm(xs, norm_final)
    return (y_prompt, y_sample, jnp.stack(k_p), jnp.stack(v_p), jnp.stack(k_s), jnp.stack(v_s),
            jnp.stack(pl_p), jnp.stack(pl_s))
```

```python
import functools

import jax
import jax.numpy as jnp
from jax import lax
from jax.experimental import pallas as pl
from jax.experimental.pallas import tpu as pltpu

N_HEADS = 16
ROPE_THETA = 500000.0
MOBA_BLOCK = 256
MOBA_TOPK = 3
POOL_WINDOWS = (2, 4, 8, 16)
POOL_HALO = 16
EPS = 1e-6
V7X_VMEM_LIMIT_BYTES = 56 * 1024 * 1024

F32 = jnp.float32
BF16 = jnp.bfloat16
NEG_INF = float("-inf")


def _cparams(*sem):
    return pltpu.CompilerParams(dimension_semantics=sem, vmem_limit_bytes=V7X_VMEM_LIMIT_BYTES)


def _rms(x, g):
    y = x * lax.rsqrt(jnp.mean(x * x, axis=-1, keepdims=True) + EPS)
    return y * g


def _tile(n, pref):
    t = min(n, pref)
    assert n % t == 0, (n, t)
    return t


def _ffn_kernel(x_ref, g_ref, wg_ref, wu_ref, wd_ref, g2_ref, *rest, final):
    if final:
        out_ref, h_ref = rest
    else:
        out_ref, hn_ref, h_ref = rest
    j = pl.program_id(1)

    @pl.when(j == 0)
    def _():
        x = x_ref[...]
        h_ref[...] = _rms(x, g_ref[...]).astype(BF16)
        out_ref[...] = x

    h = h_ref[...]
    gt = jnp.dot(h, wg_ref[...], preferred_element_type=F32)
    up = jnp.dot(h, wu_ref[...], preferred_element_type=F32)
    a = (0.5 * (gt * jax.nn.sigmoid(gt)) * up).astype(BF16)
    out_ref[...] += jnp.dot(a, wd_ref[...], preferred_element_type=F32)

    @pl.when(j == pl.num_programs(1) - 1)
    def _():
        y = out_ref[...]
        if final:
            out_ref[...] = _rms(y, g2_ref[...])
        else:
            hn_ref[...] = _rms(y, g2_ref[...]).astype(BF16)


def _half_ffn(x, g, wg, wu, wd, g2, *, final):
    m, d = x.shape
    ff = wg.shape[1]
    tm = _tile(m, 512)
    tf = _tile(ff, 512)
    row = pl.BlockSpec((tm, d), lambda i, j: (i, 0))
    vec = pl.BlockSpec((1, d), lambda i, j: (0, 0))
    out_shape = [jax.ShapeDtypeStruct((m, d), F32)]
    out_specs = [row]
    if not final:
        out_shape.append(jax.ShapeDtypeStruct((m, d), BF16))
        out_specs.append(row)
    res = pl.pallas_call(
        functools.partial(_ffn_kernel, final=final),
        grid=(m // tm, ff // tf),
        in_specs=[row, vec,
                  pl.BlockSpec((d, tf), lambda i, j: (0, j)),
                  pl.BlockSpec((d, tf), lambda i, j: (0, j)),
                  pl.BlockSpec((tf, d), lambda i, j: (j, 0)),
                  vec],
        out_specs=out_specs,
        out_shape=out_shape,
        scratch_shapes=[pltpu.VMEM((tm, d), BF16)],
        compiler_params=_cparams("parallel", "arbitrary"),
        name="half_ffn_final" if final else "half_ffn",
    )(x, g.reshape(1, d), wg, wu, wd, g2.reshape(1, d))
    return res[0] if final else (res[0], res[1])


def _rotary(z, c, a, b, hd):
    outs = []
    for s in range(z.shape[1] // hd):
        x = z[:, s * hd:(s + 1) * hd]
        left = pltpu.roll(x, hd - 16, axis=1)
        right = pltpu.roll(x, 16, axis=1)
        outs.append(x * c + left * a + right * b)
    return outs


def _proj_kernel(h_ref, w_ref, *rest, mode, hd):
    z = jnp.dot(h_ref[...], w_ref[...], preferred_element_type=F32)
    if mode == "plain":
        (o_ref,) = rest
        o_ref[...] = z
    elif mode == "v":
        o_ref, ob_ref = rest
        o_ref[...] = z
        ob_ref[...] = z.astype(BF16)
    elif mode == "q":
        c_ref, a_ref, b_ref, ob_ref = rest
        for s, r in enumerate(_rotary(z, c_ref[...], a_ref[...], b_ref[...], hd)):
            ob_ref[:, s * hd:(s + 1) * hd] = r.astype(BF16)
    else:
        c_ref, a_ref, b_ref, o_ref, ob_ref, km_ref = rest
        tm = z.shape[0]
        for s, r in enumerate(_rotary(z, c_ref[...], a_ref[...], b_ref[...], hd)):
            o_ref[:, s * hd:(s + 1) * hd] = r
            ob_ref[:, s * hd:(s + 1) * hd] = r.astype(BF16)
            if km_ref is not None:
                nblk = tm // MOBA_BLOCK
                km_ref[0, :, s * hd:(s + 1) * hd] = (
                    r.reshape(nblk, MOBA_BLOCK, hd).sum(axis=1) * (1.0 / MOBA_BLOCK))


def _project(h, w, mode, tabs=None, want_kmean=False):
    m, k = h.shape
    n = w.shape[1]
    hd = k // N_HEADS
    tm = _tile(m, 512)
    tn = _tile(n, 1024)
    grid = (m // tm, n // tn)
    in_specs = [pl.BlockSpec((tm, k), lambda i, j: (i, 0)), pl.BlockSpec((k, tn), lambda i, j: (0, j))]
    args = [h, w]
    blk = pl.BlockSpec((tm, tn), lambda i, j: (i, j))
    if mode in ("q", "k"):
        in_specs += [pl.BlockSpec((tm, hd), lambda i, j: (i, 0))] * 3
        args += list(tabs)
    if mode == "plain":
        out_shape, out_specs = [jax.ShapeDtypeStruct((m, n), F32)], [blk]
    elif mode == "v":
        out_shape = [jax.ShapeDtypeStruct((m, n), F32), jax.ShapeDtypeStruct((m, n), BF16)]
        out_specs = [blk, blk]
    elif mode == "q":
        out_shape, out_specs = [jax.ShapeDtypeStruct((m, n), BF16)], [blk]
    else:
        out_shape = [jax.ShapeDtypeStruct((m, n), F32), jax.ShapeDtypeStruct((m, n), BF16)]
        out_specs = [blk, blk]
        if want_kmean:
            nblk = tm // MOBA_BLOCK
            out_shape.append(jax.ShapeDtypeStruct((m // tm, nblk, n), F32))
            out_specs.append(pl.BlockSpec((1, nblk, tn), lambda i, j: (i, 0, j)))
    kern = functools.partial(_proj_kernel, mode=mode, hd=hd)
    if mode == "k" and not want_kmean:
        kern = functools.partial(_proj_kernel_no_kmean, hd=hd)
    return pl.pallas_call(
        kern, grid=grid, in_specs=in_specs, out_specs=out_specs, out_shape=out_shape,
        compiler_params=_cparams("parallel", "parallel"), name="proj_" + mode,
    )(*args)


def _proj_kernel_no_kmean(h_ref, w_ref, c_ref, a_ref, b_ref, o_ref, ob_ref, *, hd):
    _proj_kernel(h_ref, w_ref, c_ref, a_ref, b_ref, o_ref, ob_ref, None, mode="k", hd=hd)


def _rotary_tables(pos, hd):
    rot = hd // 4
    half = rot // 2
    inv = jnp.power(ROPE_THETA, -jnp.arange(0, rot, 2, dtype=F32) / rot)
    ang = pos.astype(F32)[:, None] * inv[None, :]
    cos, sin = jnp.cos(ang), jnp.sin(ang)
    n = pos.shape[0]
    ones = jnp.ones((n, hd - rot), F32)
    zeros = jnp.zeros((n, hd - rot), F32)
    zh = jnp.zeros((n, half), F32)
    c = jnp.concatenate([cos, cos, ones], axis=1)
    a = jnp.concatenate([-sin, zh, zeros], axis=1)
    b = jnp.concatenate([zh, sin, zeros], axis=1)
    return c, a, b


def _top_blocks(gate, limit, nb, axis):
    idx = lax.broadcasted_iota(jnp.int32, gate.shape, axis)
    g = jnp.where(idx < limit, gate, NEG_INF)
    sel = jnp.zeros(gate.shape, F32)
    picks = []
    for _ in range(min(MOBA_TOPK, nb)):
        m = jnp.max(g, axis=axis, keepdims=True)
        cand = jnp.where((g == m) & (m > NEG_INF), idx, nb)
        first = jnp.min(cand, axis=axis, keepdims=True)
        pick = idx == first
        sel = jnp.where(pick, 1.0, sel)
        g = jnp.where(pick, NEG_INF, g)
        picks.append(first)
    return sel, picks


def _moba_kernel(q_ref, k_ref, v_ref, km_ref, o_ref, *, nb, scale):
    i = pl.program_id(1)
    q = q_ref[...]
    bs = q.shape[0]
    dims = (((1,), (1,)), ((), ()))
    gate = lax.dot_general(q.astype(F32), km_ref[...], dims, precision=lax.Precision.HIGHEST,
                           preferred_element_type=F32)
    sel, _ = _top_blocks(gate, i, nb, axis=1)
    col = lax.broadcasted_iota(jnp.int32, sel.shape, 1)

    own = pl.multiple_of(i * bs, bs)
    s = lax.dot_general(q, k_ref[pl.ds(own, bs), :], dims, preferred_element_type=F32) * scale
    r_ix = lax.broadcasted_iota(jnp.int32, s.shape, 0)
    c_ix = lax.broadcasted_iota(jnp.int32, s.shape, 1)
    s = jnp.where(c_ix <= r_ix, s, NEG_INF)
    m0 = jnp.max(s, axis=1, keepdims=True)
    p = jnp.exp(s - m0)
    l0 = jnp.sum(p, axis=1, keepdims=True)
    acc0 = jnp.dot(p.astype(BF16), v_ref[pl.ds(own, bs), :], preferred_element_type=F32)

    def body(n, carry):
        m, l, acc = carry
        off = pl.multiple_of(n * bs, bs)
        s = lax.dot_general(q, k_ref[pl.ds(off, bs), :], dims, preferred_element_type=F32) * scale
        chosen = jnp.max(jnp.where(col == n, sel, 0.0), axis=1, keepdims=True)
        s = jnp.where(chosen > 0.0, s, NEG_INF)
        m_new = jnp.maximum(m, jnp.max(s, axis=1, keepdims=True))
        alpha = jnp.exp(m - m_new)
        p = jnp.exp(s - m_new)
        l = alpha * l + jnp.sum(p, axis=1, keepdims=True)
        acc = alpha * acc + jnp.dot(p.astype(BF16), v_ref[pl.ds(off, bs), :], preferred_element_type=F32)
        return m_new, l, acc

    m, l, acc = lax.fori_loop(0, i, body, (m0, l0, acc0))
    o_ref[...] = (acc / l).astype(BF16)


def _moba_prompt(q, k, v, kmean):
    l, w = q.shape
    hd = w // N_HEADS
    nb = l // MOBA_BLOCK
    return pl.pallas_call(
        functools.partial(_moba_kernel, nb=nb, scale=hd ** -0.5),
        grid=(N_HEADS, nb),
        in_specs=[pl.BlockSpec((MOBA_BLOCK, hd), lambda h, i: (i, h)),
                  pl.BlockSpec((l, hd), lambda h, i: (0, h)),
                  pl.BlockSpec((l, hd), lambda h, i: (0, h)),
                  pl.BlockSpec((nb, hd), lambda h, i: (0, h))],
        out_specs=pl.BlockSpec((MOBA_BLOCK, hd), lambda h, i: (i, h)),
        out_shape=jax.ShapeDtypeStruct((l, w), BF16),
        compiler_params=_cparams("parallel", "arbitrary"),
        name="moba_prompt",
    )(q, k, v, kmean)


def _mix_kernel(att_ref, u_ref, halo_ref, ga_ref, gb_ref, wa_ref, wp_ref, pm_ref, ps_ref, o_ref,
                ext_ref, mixed_ref, *, pos0, zero_first_halo):
    i = pl.program_id(0)
    j = pl.program_id(1)
    tm = u_ref.shape[0]
    pw = u_ref.shape[1]
    gw = pw // len(POOL_WINDOWS)

    @pl.when(j == 0)
    def _():
        halo = halo_ref[...]
        if zero_first_halo:
            halo = jnp.where(i == 0, 0.0, halo)
        ext_ref[0:POOL_HALO, :] = halo
        ext_ref[POOL_HALO:, :] = u_ref[...]
        pos = pos0 + i * tm + lax.broadcasted_iota(jnp.int32, (tm, 1), 0)
        for g, wdw in enumerate(POOL_WINDOWS):
            cols = slice(g * gw, (g + 1) * gw)
            tot = ext_ref[POOL_HALO:, cols]
            for d in range(1, wdw):
                tot = tot + ext_ref[POOL_HALO - d:POOL_HALO - d + tm, cols]
            cnt = jnp.minimum(wdw, pos + 1).astype(F32)
            pooled = tot / cnt - ext_ref[POOL_HALO:, cols]
            mixed = jnp.dot(pooled.astype(BF16), pm_ref[g], preferred_element_type=F32)
            mixed_ref[:, cols] = (mixed * ps_ref[:, cols]).astype(BF16)

    a = jnp.dot(att_ref[...], wa_ref[...], preferred_element_type=F32)
    b = jnp.dot(mixed_ref[...], wp_ref[...], preferred_element_type=F32)
    o_ref[...] = (jax.nn.sigmoid(ga_ref[...]) * a + jax.nn.sigmoid(gb_ref[...]) * b).astype(BF16)


def _branch_mix(att, zc, halo_src, wa, wp, pool_map, pool_scale, *, pos0, halo_is_state):
    m, d = att.shape
    pw = wp.shape[0]
    tn = _tile(d, 1024)
    nu = pw // tn if pw >= tn else None
    assert pw % tn == 0
    if halo_is_state:
        return _branch_mix_rows(att, zc, halo_src, wa, wp, pool_map, pool_scale)
    tm = _tile(m, 512)
    hb = tm // POOL_HALO
    return pl.pallas_call(
        functools.partial(_mix_kernel, pos0=pos0, zero_first_halo=True),
        grid=(m // tm, d // tn),
        in_specs=[pl.BlockSpec((tm, d), lambda i, j: (i, 0)),
                  pl.BlockSpec((tm, pw), lambda i, j: (i, 0)),
                  pl.BlockSpec((POOL_HALO, pw), lambda i, j: (jnp.maximum(i * hb - 1, 0), 0)),
                  pl.BlockSpec((tm, tn), lambda i, j: (i, nu + j)),
                  pl.BlockSpec((tm, tn), lambda i, j: (i, nu + d // tn + j)),
                  pl.BlockSpec((d, tn), lambda i, j: (0, j)),
                  pl.BlockSpec((pw, tn), lambda i, j: (0, j)),
                  pl.BlockSpec(pool_map.shape, lambda i, j: (0, 0, 0)),
                  pl.BlockSpec((1, pw), lambda i, j: (0, 0))],
        out_specs=pl.BlockSpec((tm, tn), lambda i, j: (i, j)),
        out_shape=jax.ShapeDtypeStruct((m, d), BF16),
        scratch_shapes=[pltpu.VMEM((tm + POOL_HALO, pw), F32), pltpu.VMEM((tm, pw), BF16)],
        compiler_params=_cparams("parallel", "arbitrary"),
        name="branch_mix",
    )(att, zc, zc, zc, zc, wa, wp, pool_map, pool_scale.reshape(1, pw))


def _mix_rows_kernel(att_ref, zc_ref, st_ref, wa_ref, wp_ref, pm_ref, ps_ref, o_ref, mixed_ref):
    nbuf, _, pw = st_ref.shape
    d = att_ref.shape[1]
    gw = pw // len(POOL_WINDOWS)
    u = zc_ref[:, 0:pw]
    for g, wdw in enumerate(POOL_WINDOWS):
        cols = slice(g * gw, (g + 1) * gw)
        tot = u[:, cols]
        for r in range(nbuf - (wdw - 1), nbuf):
            tot = tot + st_ref[r, :, cols]
        pooled = tot / float(wdw) - u[:, cols]
        mixed = jnp.dot(pooled.astype(BF16), pm_ref[g], preferred_element_type=F32)
        mixed_ref[:, cols] = (mixed * ps_ref[:, cols]).astype(BF16)
    a = jnp.dot(att_ref[...], wa_ref[...], preferred_element_type=F32)
    b = jnp.dot(mixed_ref[...], wp_ref[...], preferred_element_type=F32)
    ga = zc_ref[:, pw:pw + d]
    gb = zc_ref[:, pw + d:pw + 2 * d]
    o_ref[...] = (jax.nn.sigmoid(ga) * a + jax.nn.sigmoid(gb) * b).astype(BF16)


def _branch_mix_rows(att, zc, state, wa, wp, pool_map, pool_scale):
    m, d = att.shape
    pw = wp.shape[0]
    full = lambda a: pl.BlockSpec(a.shape, lambda i: (0,) * a.ndim)
    ps = pool_scale.reshape(1, pw)
    args = (att, zc, state, wa, wp, pool_map, ps)
    return pl.pallas_call(
        _mix_rows_kernel, grid=(1,), in_specs=[full(a) for a in args],
        out_specs=pl.BlockSpec((m, d), lambda i: (0, 0)),
        out_shape=jax.ShapeDtypeStruct((m, d), BF16),
        scratch_shapes=[pltpu.VMEM((m, pw), BF16)],
        compiler_params=_cparams("arbitrary"),
        name="branch_mix_rows",
    )(*args)


def _outproj_kernel(m_ref, w_ref, x_ref, o_ref):
    o_ref[...] = x_ref[...] + jnp.dot(m_ref[...], w_ref[...], preferred_element_type=F32)


def _out_project(mix, w, x):
    m, d = x.shape
    tm = _tile(m, 512)
    tn = _tile(d, 1024)
    return pl.pallas_call(
        _outproj_kernel, grid=(m // tm, d // tn),
        in_specs=[pl.BlockSpec((tm, d), lambda i, j: (i, 0)),
                  pl.BlockSpec((d, tn), lambda i, j: (0, j)),
                  pl.BlockSpec((tm, tn), lambda i, j: (i, j))],
        out_specs=pl.BlockSpec((tm, tn), lambda i, j: (i, j)),
        out_shape=jax.ShapeDtypeStruct((m, d), F32),
        compiler_params=_cparams("parallel", "parallel"),
        name="out_project",
    )(mix, w, x)


def _page_mean_kernel(pt_ref, *refs, ppb):
    o_ref = refs[ppb]
    n = pl.program_id(1)
    tot = jnp.sum(refs[0][0], axis=0)
    for r in refs[1:ppb]:
        tot = tot + jnp.sum(r[0], axis=0)
    o_ref[0, n] = tot * (1.0 / MOBA_BLOCK)


def _paged_block_means(cache, page_table, nb):
    n_phys, page, h, hd = cache.shape
    b, n_pages = page_table.shape
    ppb = MOBA_BLOCK // page
    assert MOBA_BLOCK == ppb * page and n_pages == ppb * nb

    def spec(part):
        return pl.BlockSpec((1, page, h, hd), lambda bi, n, pt: (pt[bi * n_pages + ppb * n + part], 0, 0, 0))

    return pl.pallas_call(
        functools.partial(_page_mean_kernel, ppb=ppb),
        grid_spec=pltpu.PrefetchScalarGridSpec(
            num_scalar_prefetch=1, grid=(b, nb),
            in_specs=[spec(part) for part in range(ppb)],
            out_specs=pl.BlockSpec((1, nb, h, hd), lambda bi, n, pt: (bi, 0, 0, 0))),
        out_shape=jax.ShapeDtypeStruct((b, nb, h, hd), F32),
        compiler_params=_cparams("parallel", "arbitrary"),
        name="paged_block_means",
    )(page_table.reshape(-1), *([cache] * ppb))


def _sample_gate_kernel(q_ref, km_ref, o_ref, *, nb):
    gate = jnp.sum(km_ref[0] * q_ref[...], axis=-1, keepdims=True)
    _, picks = _top_blocks(gate, nb, nb, axis=0)
    for t, first in enumerate(picks):
        o_ref[0, t] = first[0]


def _sample_gate(q, kmean):
    b, nb, h, hd = kmean.shape
    n_sel = min(MOBA_TOPK, nb)
    return pl.pallas_call(
        functools.partial(_sample_gate_kernel, nb=nb),
        grid=(b,),
        in_specs=[pl.BlockSpec((1, h, hd), lambda i: (i, 0, 0)),
                  pl.BlockSpec((1, nb, h, hd), lambda i: (i, 0, 0, 0))],
        out_specs=pl.BlockSpec((1, n_sel, h, 1), lambda i: (i, 0, 0, 0)),
        out_shape=jax.ShapeDtypeStruct((b, n_sel, h, 1), jnp.int32),
        compiler_params=_cparams("parallel"),
        name="sample_gate",
    )(q, kmean)


def _paged_attn_kernel(sel_ref, pt_ref, q_ref, kn_ref, vn_ref, ck_ref, cv_ref, o_ref, kbuf, vbuf, sems,
                       *, scale, n_sel, n_pages, ppb):
    b = pl.program_id(0)
    nbatch = pl.num_programs(0)
    h_all, page = ck_ref.shape[2], ck_ref.shape[1]

    def copies(bi, slot):
        out = []
        for h in range(h_all):
            for t in range(n_sel):
                blk = sel_ref[(bi * n_sel + t) * h_all + h]
                for part in range(ppb):
                    pg = pt_ref[bi * n_pages + blk * ppb + part]
                    rows = pl.ds((t * ppb + part) * page, page)
                    out.append(pltpu.make_async_copy(ck_ref.at[pg, :, h, :], kbuf.at[slot, h, rows, :],
                                                     sems.at[slot, 0]))
                    out.append(pltpu.make_async_copy(cv_ref.at[pg, :, h, :], vbuf.at[slot, h, rows, :],
                                                     sems.at[slot, 1]))
        return out

    slot = b % 2

    @pl.when(b == 0)
    def _():
        for c in copies(b, slot):
            c.start()

    @pl.when(b + 1 < nbatch)
    def _():
        for c in copies(b + 1, 1 - slot):
            c.start()

    for c in copies(b, slot):
        c.wait()

    for h in range(h_all):
        q = q_ref[0, h:h + 1, :]
        kn = kn_ref[0, h:h + 1, :]
        vn = vn_ref[0, h:h + 1, :]
        kp = kbuf[slot, h].astype(BF16)
        vp = vbuf[slot, h].astype(BF16)
        q8 = jnp.broadcast_to(q, (8, q.shape[1])).astype(BF16)
        s = lax.dot_general(q8, kp, (((1,), (1,)), ((), ())), preferred_element_type=F32)[0:1] * scale
        s0 = jnp.sum(q * kn, axis=1, keepdims=True) * scale
        m = jnp.maximum(jnp.max(s, axis=1, keepdims=True), s0)
        pr = jnp.exp(s - m)
        p0 = jnp.exp(s0 - m)
        l = jnp.sum(pr, axis=1, keepdims=True) + p0
        pv = jnp.dot(jnp.broadcast_to(pr, (8, pr.shape[1])).astype(BF16), vp, preferred_element_type=F32)[0:1]
        o_ref[0, h:h + 1, :] = (pv + p0 * vn) / l


def _paged_attention(q, k_new, v_new, cache_k, cache_v, sel, page_table):
    b, h, hd = q.shape
    n_sel = sel.shape[1]
    n_pages = page_table.shape[1]
    page = cache_k.shape[1]
    ppb = MOBA_BLOCK // page
    tok = pl.BlockSpec((1, h, hd), lambda bi, s, t: (bi, 0, 0))
    hbm = pl.BlockSpec(memory_space=pl.ANY)
    return pl.pallas_call(
        functools.partial(_paged_attn_kernel, scale=hd ** -0.5, n_sel=n_sel, n_pages=n_pages, ppb=ppb),
        grid_spec=pltpu.PrefetchScalarGridSpec(
            num_scalar_prefetch=2, grid=(b,),
            in_specs=[tok, tok, tok, hbm, hbm],
            out_specs=tok,
            scratch_shapes=[pltpu.VMEM((2, h, n_sel * MOBA_BLOCK, hd), F32),
                            pltpu.VMEM((2, h, n_sel * MOBA_BLOCK, hd), F32),
                            pltpu.SemaphoreType.DMA((2, 2))]),
        out_shape=jax.ShapeDtypeStruct((b, h, hd), F32),
        compiler_params=_cparams("arbitrary"),
        name="paged_attention",
    )(sel.reshape(-1), page_table.reshape(-1), q, k_new, v_new, cache_k, cache_v)


def _layer(x, pos, weights, attend, *, pos0, pool_prefix):
    (g1, f1g, f1u, f1d, gm, w_q, w_k, w_v, w_c, pool_map, pool_scale, wa, wp, wo, g2, f2g, f2u, f2d, gf) = weights
    m, d = x.shape
    hd = d // N_HEADS
    x1, h = _half_ffn(x, g1, f1g, f1u, f1d, gm, final=False)
    tabs = _rotary_tables(pos, hd)
    (q,) = _project(h, w_q, "q", tabs)
    want_kmean = pool_prefix is None
    kres = _project(h, w_k, "k", tabs, want_kmean=want_kmean)
    k, kb = kres[0], kres[1]
    kmean = kres[2].reshape(-1, d) if want_kmean else None
    v, vb = _project(h, w_v, "v")
    (zc,) = _project(h, w_c, "plain")
    att = attend(q, kb, vb, kmean)
    mix = _branch_mix(att, zc, zc if pool_prefix is None else pool_prefix, wa, wp, pool_map, pool_scale,
                      pos0=pos0, halo_is_state=pool_prefix is not None)
    x2 = _out_project(mix, wo, x1)
    y = _half_ffn(x2, g2, f2g, f2u, f2d, gf, final=True)
    return y, k, v, zc[:, :wp.shape[0]]


def kernel(x_prompt, x_sample, cache_k, cache_v, page_table, state_pool, norm_ffn1, ffn1_gate, ffn1_up, ffn1_down,
           norm_mix, w_in, pool_map, pool_scale, w_branch_attn, w_branch_pool, w_out, norm_ffn2, ffn2_gate,
           ffn2_up, ffn2_down, norm_final):
    batch, seq, d = x_prompt.shape
    dec_batch, dec_seq, _ = x_sample.shape
    depth, n_phys, page, _, hd = cache_k.shape
    n_pages = page_table.shape[1]
    past_len = n_pages * page
    pw = pool_scale.shape[1]
    nbuf = state_pool.shape[2]
    assert batch == 1 and dec_seq == 1 and depth == 1 and d == N_HEADS * hd
    assert seq % MOBA_BLOCK == 0 and past_len % MOBA_BLOCK == 0 and nbuf == max(POOL_WINDOWS) - 1
    assert past_len + 1 >= max(POOL_WINDOWS)

    lyr = 0
    wb = lambda a: a.astype(BF16)
    w_in_l = w_in[lyr]
    weights = (norm_ffn1[lyr], wb(ffn1_gate[lyr]), wb(ffn1_up[lyr]), wb(ffn1_down[lyr]), norm_mix[lyr],
               wb(w_in_l[:, :d]), wb(w_in_l[:, d:2 * d]), wb(w_in_l[:, 2 * d:3 * d]), wb(w_in_l[:, 3 * d:]),
               wb(pool_map[lyr]), pool_scale[lyr], wb(w_branch_attn[lyr]), wb(w_branch_pool[lyr]), wb(w_out[lyr]),
               norm_ffn2[lyr], wb(ffn2_gate[lyr]), wb(ffn2_up[lyr]), wb(ffn2_down[lyr]), norm_final)

    pos_p = jnp.arange(seq, dtype=jnp.int32)
    y_p, k_p, v_p, u_p = _layer(x_prompt[0], pos_p, weights, _moba_prompt, pos0=0, pool_prefix=None)

    ck, cv = cache_k[lyr], cache_v[lyr]
    nb_past = past_len // MOBA_BLOCK

    def attend_sample(q, kb, vb, _):
        tok = lambda a: a.astype(F32).reshape(dec_batch, N_HEADS, hd)
        kmean = _paged_block_means(ck, page_table, nb_past)
        sel = _sample_gate(tok(q), kmean)
        att = _paged_attention(tok(q), tok(kb), tok(vb), ck, cv, sel[..., 0], page_table)
        return att.reshape(dec_batch, d).astype(BF16)

    pos_s = jnp.full((dec_batch,), past_len, dtype=jnp.int32)
    y_s, k_s, v_s, u_s = _layer(x_sample[:, 0], pos_s, weights, attend_sample, pos0=past_len,
                                pool_prefix=jnp.swapaxes(state_pool[lyr], 0, 1))

    heads = lambda a, b_, l_: a.reshape(1, b_, l_, N_HEADS, hd)
    pool_p = u_p[seq - nbuf:].reshape(1, 1, nbuf, pw)
    pool_s = jnp.concatenate([state_pool[lyr][:, 1:], u_s[:, None, :]], axis=1)[None]
    return (y_p[None], y_s[:, None], heads(k_p, 1, seq), heads(v_p, 1, seq),
            heads(k_s, dec_batch, 1), heads(v_s, dec_batch, 1), pool_p, pool_s)
```

```python
import functools

import jax
import jax.numpy as jnp
from jax import lax
from jax.experimental import pallas as pl
from jax.experimental.pallas import tpu as pltpu

N_HEADS = 16
ROPE_THETA = 500000.0
MOBA_BLOCK = 256
MOBA_TOPK = 3
POOL_WINDOWS = (2, 4, 8, 16)
POOL_HALO = 16
EPS = 1e-6
V7X_VMEM_LIMIT_BYTES = 56 * 1024 * 1024
KV_GROUP = 4
PAGE_LANES = 8
LOG2E = 1.4426950408889634

F32 = jnp.float32
BF16 = jnp.bfloat16
NEG_INF = float("-inf")
POS_INF = float("inf")


def _cparams(*sem):
    return pltpu.CompilerParams(dimension_semantics=sem, vmem_limit_bytes=V7X_VMEM_LIMIT_BYTES)


def _rms(x, g):
    y = x * lax.rsqrt(jnp.mean(x * x, axis=-1, keepdims=True) + EPS)
    return y * g


def _tile(n, pref):
    t = min(n, pref)
    assert n % t == 0, (n, t)
    return t


CAST_BLOCK_BYTES = 6 * 1024 * 1024


def _cast_kernel(w_ref, o_ref):
    o_ref[...] = w_ref[...].astype(BF16)


def _to_bf16(w):
    r, c = w.shape
    tr = r
    while tr * c * 4 > CAST_BLOCK_BYTES and tr % 32 == 0:
        tr //= 2
    return pl.pallas_call(
        _cast_kernel, grid=(r // tr,),
        in_specs=[pl.BlockSpec((tr, c), lambda i: (i, 0))],
        out_specs=pl.BlockSpec((tr, c), lambda i: (i, 0)),
        out_shape=jax.ShapeDtypeStruct((r, c), BF16),
        compiler_params=_cparams("parallel"),
        name="to_bf16",
    )(w)


def _ffn_kernel(x_ref, g_ref, wg_ref, wu_ref, wd_ref, g2_ref, *rest, final):
    if final:
        out_ref, h_ref = rest
    else:
        out_ref, hn_ref, h_ref = rest
    j = pl.program_id(1)

    @pl.when(j == 0)
    def _():
        x = x_ref[...]
        h_ref[...] = _rms(x, g_ref[...]).astype(BF16)
        out_ref[...] = x

    h = h_ref[...]
    gt = jnp.dot(h, wg_ref[...], preferred_element_type=F32)
    up = jnp.dot(h, wu_ref[...], preferred_element_type=F32)
    a = (0.5 * (gt * jax.nn.sigmoid(gt)) * up).astype(BF16)
    out_ref[...] += jnp.dot(a, wd_ref[...], preferred_element_type=F32)

    @pl.when(j == pl.num_programs(1) - 1)
    def _():
        y = out_ref[...]
        if final:
            out_ref[...] = _rms(y, g2_ref[...])
        else:
            hn_ref[...] = _rms(y, g2_ref[...]).astype(BF16)


def _half_ffn(x, g, wg, wu, wd, g2, *, final):
    m, d = x.shape
    ff = wg.shape[1]
    tm = _tile(m, 512)
    tf = _tile(ff, 512)
    row = pl.BlockSpec((tm, d), lambda i, j: (i, 0))
    vec = pl.BlockSpec((1, d), lambda i, j: (0, 0))
    out_shape = [jax.ShapeDtypeStruct((m, d), F32)]
    out_specs = [row]
    if not final:
        out_shape.append(jax.ShapeDtypeStruct((m, d), BF16))
        out_specs.append(row)
    res = pl.pallas_call(
        functools.partial(_ffn_kernel, final=final),
        grid=(m // tm, ff // tf),
        in_specs=[row, vec,
                  pl.BlockSpec((d, tf), lambda i, j: (0, j)),
                  pl.BlockSpec((d, tf), lambda i, j: (0, j)),
                  pl.BlockSpec((tf, d), lambda i, j: (j, 0)),
                  vec],
        out_specs=out_specs,
        out_shape=out_shape,
        scratch_shapes=[pltpu.VMEM((tm, d), BF16)],
        compiler_params=_cparams("parallel", "arbitrary"),
        name="half_ffn_final" if final else "half_ffn",
    )(x, g.reshape(1, d), wg, wu, wd, g2.reshape(1, d))
    return res[0] if final else (res[0], res[1])


def _rotary(z, c, a, b, hd):
    outs = []
    for s in range(z.shape[1] // hd):
        x = z[:, s * hd:(s + 1) * hd]
        left = pltpu.roll(x, hd - 16, axis=1)
        right = pltpu.roll(x, 16, axis=1)
        outs.append(x * c + left * a + right * b)
    return outs


def _proj_kernel(h_ref, w_ref, *rest, mode, hd):
    z = jnp.dot(h_ref[...], w_ref[...], preferred_element_type=F32)
    if mode == "plain":
        (o_ref,) = rest
        o_ref[...] = z
    elif mode == "v":
        o_ref, ob_ref = rest
        o_ref[...] = z
        ob_ref[...] = z.astype(BF16)
    elif mode == "q":
        c_ref, a_ref, b_ref, ob_ref = rest
        for s, r in enumerate(_rotary(z, c_ref[...], a_ref[...], b_ref[...], hd)):
            ob_ref[:, s * hd:(s + 1) * hd] = r.astype(BF16)
    else:
        c_ref, a_ref, b_ref, o_ref, ob_ref, km_ref = rest
        tm = z.shape[0]
        for s, r in enumerate(_rotary(z, c_ref[...], a_ref[...], b_ref[...], hd)):
            o_ref[:, s * hd:(s + 1) * hd] = r
            ob_ref[:, s * hd:(s + 1) * hd] = r.astype(BF16)
            if km_ref is not None:
                nblk = tm // MOBA_BLOCK
                km_ref[0, :, s * hd:(s + 1) * hd] = (
                    r.reshape(nblk, MOBA_BLOCK, hd).sum(axis=1) * (1.0 / MOBA_BLOCK))


def _project(h, w, col0, n, mode, tabs=None, want_kmean=False):
    m, k = h.shape
    hd = k // N_HEADS
    tm = _tile(m, 1024)
    tn = _tile(n, 1024)
    assert col0 % tn == 0
    cb0 = col0 // tn
    grid = (m // tm, n // tn)
    in_specs = [pl.BlockSpec((tm, k), lambda i, j: (i, 0)), pl.BlockSpec((k, tn), lambda i, j: (0, cb0 + j))]
    args = [h, w]
    blk = pl.BlockSpec((tm, tn), lambda i, j: (i, j))
    if mode in ("q", "k"):
        in_specs += [pl.BlockSpec((tm, hd), lambda i, j: (i, 0))] * 3
        args += list(tabs)
    if mode == "plain":
        out_shape, out_specs = [jax.ShapeDtypeStruct((m, n), F32)], [blk]
    elif mode == "v":
        out_shape = [jax.ShapeDtypeStruct((m, n), F32), jax.ShapeDtypeStruct((m, n), BF16)]
        out_specs = [blk, blk]
    elif mode == "q":
        out_shape, out_specs = [jax.ShapeDtypeStruct((m, n), BF16)], [blk]
    else:
        out_shape = [jax.ShapeDtypeStruct((m, n), F32), jax.ShapeDtypeStruct((m, n), BF16)]
        out_specs = [blk, blk]
        if want_kmean:
            nblk = tm // MOBA_BLOCK
            out_shape.append(jax.ShapeDtypeStruct((m // tm, nblk, n), F32))
            out_specs.append(pl.BlockSpec((1, nblk, tn), lambda i, j: (i, 0, j)))
    kern = functools.partial(_proj_kernel, mode=mode, hd=hd)
    if mode == "k" and not want_kmean:
        kern = functools.partial(_proj_kernel_no_kmean, hd=hd)
    return pl.pallas_call(
        kern, grid=grid, in_specs=in_specs, out_specs=out_specs, out_shape=out_shape,
        compiler_params=_cparams("parallel", "parallel"), name="proj_" + mode,
    )(*args)


def _proj_kernel_no_kmean(h_ref, w_ref, c_ref, a_ref, b_ref, o_ref, ob_ref, *, hd):
    _proj_kernel(h_ref, w_ref, c_ref, a_ref, b_ref, o_ref, ob_ref, None, mode="k", hd=hd)


def _rotary_tables(pos, hd):
    rot = hd // 4
    half = rot // 2
    inv = jnp.power(ROPE_THETA, -jnp.arange(0, rot, 2, dtype=F32) / rot)
    ang = pos.astype(F32)[:, None] * inv[None, :]
    cos, sin = jnp.cos(ang), jnp.sin(ang)
    n = pos.shape[0]
    ones = jnp.ones((n, hd - rot), F32)
    zeros = jnp.zeros((n, hd - rot), F32)
    zh = jnp.zeros((n, half), F32)
    c = jnp.concatenate([cos, cos, ones], axis=1)
    a = jnp.concatenate([-sin, zh, zeros], axis=1)
    b = jnp.concatenate([zh, sin, zeros], axis=1)
    return c, a, b


def _top_blocks(gate, limit, nb, axis):
    idx = lax.broadcasted_iota(jnp.int32, gate.shape, axis)
    g = jnp.where(idx < limit, gate, NEG_INF)
    sel = jnp.zeros(gate.shape, F32)
    picks = []
    for _ in range(min(MOBA_TOPK, nb)):
        m = jnp.max(g, axis=axis, keepdims=True)
        cand = jnp.where((g == m) & (m > NEG_INF), idx, nb)
        first = jnp.min(cand, axis=axis, keepdims=True)
        pick = idx == first
        sel = jnp.where(pick, 1.0, sel)
        g = jnp.where(pick, NEG_INF, g)
        picks.append(first)
    return sel, picks


def _moba_kernel(pt_ref, q_ref, k_ref, v_ref, sel_ref, *rest, nb, c, group, n_page_refs, ppb, n_page_steps):
    page_refs = rest[:n_page_refs]
    o_ref, pm_ref, vt_ref = rest[n_page_refs:]
    i = pl.program_id(1)
    bs = q_ref.shape[0]

    def page_means():
        for blk in range(n_page_refs // ppb):
            tot = None
            for r in page_refs[blk * ppb:(blk + 1) * ppb]:
                x = r[0]
                part = x.reshape(PAGE_LANES, x.shape[0] // PAGE_LANES, *x.shape[1:]).sum(axis=1)
                tot = part if tot is None else tot + part
            pm_ref[0, blk] = tot.sum(axis=0) * (1.0 / MOBA_BLOCK)

    @pl.when(i == 0)
    def _():
        for n in range(nb):
            rows = slice(n * bs, (n + 1) * bs)
            vt_ref[:, rows] = v_ref[rows, :].astype(F32).T.astype(BF16)

    qtb = q_ref[...].astype(F32).T.astype(BF16)
    if n_page_steps >= pl.num_programs(0) * nb:
        page_means()
    else:
        pl.when(pl.program_id(0) * nb + i < n_page_steps)(page_means)
    key_ix = lax.broadcasted_iota(jnp.int32, (bs, bs), 0)
    qry_ix = lax.broadcasted_iota(jnp.int32, (bs, bs), 1)
    future = key_ix > qry_ix

    def block_of(trip, g):
        j = trip * group + g
        if g == 0:
            is_own = trip == 0
            return jnp.where(is_own, i, jnp.minimum(j - 1, nb - 1)), is_own
        return jnp.minimum(j - 1, nb - 1), None

    def visible(n, is_own):
        ch = sel_ref[pl.ds(n, 1), :] > 0.0
        return ch if is_own is None else jnp.logical_or(ch, is_own)

    def score_stage(trip):
        ts, mxs = [], []
        for g in range(group):
            n, is_own = block_of(trip, g)
            off = pl.multiple_of(n * bs, bs)
            t = jnp.dot(k_ref[pl.ds(off, bs), :], qtb, preferred_element_type=F32) * c
            if is_own is not None:
                t = jnp.where(jnp.logical_and(future, is_own), NEG_INF, t)
            ts.append(t)
            mxs.append(jnp.where(visible(n, is_own), jnp.max(t, axis=0, keepdims=True), NEG_INF))
        return tuple(ts), tuple(mxs)

    def body(trip, carry):
        (m, l, acc), ts, mxs = carry
        nxt = score_stage(trip + 1)
        m_new = m
        for mx in mxs:
            m_new = jnp.maximum(m_new, mx)
        alpha = jnp.exp2(m - m_new)
        l = alpha * l
        acc = alpha * acc
        for g in range(group):
            n, is_own = block_of(trip, g)
            off = pl.multiple_of(n * bs, bs)
            p = jnp.exp2(ts[g] - jnp.where(visible(n, is_own), m_new, POS_INF))
            l = l + jnp.sum(p, axis=0, keepdims=True)
            acc = acc + jnp.dot(vt_ref[:, pl.ds(off, bs)], p.astype(BF16), preferred_element_type=F32)
        return ((m_new, l, acc),) + nxt

    hd = qtb.shape[0]
    init = (jnp.full((1, bs), NEG_INF, F32), jnp.zeros((1, bs), F32), jnp.zeros((hd, bs), F32))
    (m, l, acc), _, _ = lax.fori_loop(0, (i + group) // group, body, (init,) + score_stage(0))
    o_ref[...] = (acc / l).T.astype(BF16)


def _gate_kernel(q_ref, km_ref, o_ref, *, nb):
    j = pl.program_id(1)
    tq = q_ref.shape[0]
    gate = lax.dot_general(km_ref[...], q_ref[...].astype(F32), (((1,), (1,)), ((), ())),
                           precision=lax.Precision.HIGHEST, preferred_element_type=F32)
    own = (j * tq + lax.broadcasted_iota(jnp.int32, gate.shape, 1)) // MOBA_BLOCK
    sel, _ = _top_blocks(gate, own, nb, axis=0)
    o_ref[0] = sel


def _moba_select(q, kmean):
    l, w = q.shape
    hd = w // N_HEADS
    nb = l // MOBA_BLOCK
    tq = _tile(l, 2048)
    return pl.pallas_call(
        functools.partial(_gate_kernel, nb=nb),
        grid=(N_HEADS, l // tq),
        in_specs=[pl.BlockSpec((tq, hd), lambda h, j: (j, h)),
                  pl.BlockSpec((nb, hd), lambda h, j: (0, h))],
        out_specs=pl.BlockSpec((1, nb, tq), lambda h, j: (h, 0, j)),
        out_shape=jax.ShapeDtypeStruct((N_HEADS, nb, l), F32),
        compiler_params=_cparams("parallel", "parallel"),
        name="moba_select",
    )(q, kmean)


def _moba_prompt(q, k, v, kmean, cache, page_table):
    l, w = q.shape
    hd = w // N_HEADS
    nb = l // MOBA_BLOCK
    n_phys, page, h_c, hd_c = cache.shape
    b, n_pages = page_table.shape
    ppb = MOBA_BLOCK // page
    steps = N_HEADS * nb
    total_pages = b * n_pages
    pps = ppb * pl.cdiv(pl.cdiv(total_pages, steps), ppb)
    bps = pps // ppb
    nb_past = n_pages // ppb
    assert MOBA_BLOCK == ppb * page and n_pages % pps == 0
    n_page_steps = total_pages // pps
    assert n_page_steps <= steps
    groups_per_seq = nb_past // bps
    sel = _moba_select(q, kmean)

    def page_spec(part):
        def imap(h, i, pt):
            s = jnp.minimum(h * nb + i, n_page_steps - 1)
            return (pt[s * pps + part], 0, 0, 0)
        return pl.BlockSpec((1, page, h_c, hd_c), imap)

    def mean_map(h, i, pt):
        s = jnp.minimum(h * nb + i, n_page_steps - 1)
        return (s // groups_per_seq, s % groups_per_seq, 0, 0)

    return pl.pallas_call(
        functools.partial(_moba_kernel, nb=nb, c=hd ** -0.5 * LOG2E, group=KV_GROUP, n_page_refs=pps, ppb=ppb,
                          n_page_steps=n_page_steps),
        grid_spec=pltpu.PrefetchScalarGridSpec(
            num_scalar_prefetch=1, grid=(N_HEADS, nb),
            in_specs=[pl.BlockSpec((MOBA_BLOCK, hd), lambda h, i, pt: (i, h)),
                      pl.BlockSpec((l, hd), lambda h, i, pt: (0, h)),
                      pl.BlockSpec((l, hd), lambda h, i, pt: (0, h)),
                      pl.BlockSpec((None, nb, MOBA_BLOCK), lambda h, i, pt: (h, 0, i))]
                     + [page_spec(part) for part in range(pps)],
            out_specs=[pl.BlockSpec((MOBA_BLOCK, hd), lambda h, i, pt: (i, h)),
                       pl.BlockSpec((1, bps, h_c, hd_c), mean_map)],
            scratch_shapes=[pltpu.VMEM((hd, l), BF16)]),
        out_shape=[jax.ShapeDtypeStruct((l, w), BF16), jax.ShapeDtypeStruct((b, nb_past, h_c, hd_c), F32)],
        compiler_params=_cparams("arbitrary", "arbitrary"),
        name="moba_prompt",
    )(page_table.reshape(-1), q, k, v, sel, *([cache] * pps))


def _mix_kernel(att_ref, u_ref, halo_ref, ga_ref, gb_ref, wa_ref, wp_ref, pm_ref, ps_ref, o_ref,
                ext_ref, mixed_ref, *, pos0, zero_first_halo):
    i = pl.program_id(0)
    j = pl.program_id(1)
    tm = u_ref.shape[0]
    pw = u_ref.shape[1]
    gw = pw // len(POOL_WINDOWS)

    @pl.when(j == 0)
    def _():
        halo = halo_ref[...]
        if zero_first_halo:
            halo = jnp.where(i == 0, 0.0, halo)
        ext_ref[0:POOL_HALO, :] = halo
        ext_ref[POOL_HALO:, :] = u_ref[...]
        pos = pos0 + i * tm + lax.broadcasted_iota(jnp.int32, (tm, 1), 0)
        for g, wdw in enumerate(POOL_WINDOWS):
            cols = slice(g * gw, (g + 1) * gw)
            tot = ext_ref[POOL_HALO:, cols]
            for d in range(1, wdw):
                tot = tot + ext_ref[POOL_HALO - d:POOL_HALO - d + tm, cols]
            cnt = jnp.minimum(wdw, pos + 1).astype(F32)
            pooled = tot / cnt - ext_ref[POOL_HALO:, cols]
            mixed = jnp.dot(pooled.astype(BF16), pm_ref[g], preferred_element_type=F32)
            mixed_ref[:, cols] = (mixed * ps_ref[:, cols]).astype(BF16)

    a = jnp.dot(att_ref[...], wa_ref[...], preferred_element_type=F32)
    b = jnp.dot(mixed_ref[...], wp_ref[...], preferred_element_type=F32)
    o_ref[...] = (jax.nn.sigmoid(ga_ref[...]) * a + jax.nn.sigmoid(gb_ref[...]) * b).astype(BF16)


def _branch_mix(att, zc, halo_src, wa, wp, pool_map, pool_scale, *, pos0, halo_is_state):
    m, d = att.shape
    pw = wp.shape[0]
    tn = _tile(d, 1024)
    nu = pw // tn if pw >= tn else None
    assert pw % tn == 0
    if halo_is_state:
        return _branch_mix_rows(att, zc, halo_src, wa, wp, pool_map, pool_scale)
    tm = _tile(m, 512)
    hb = tm // POOL_HALO
    return pl.pallas_call(
        functools.partial(_mix_kernel, pos0=pos0, zero_first_halo=True),
        grid=(m // tm, d // tn),
        in_specs=[pl.BlockSpec((tm, d), lambda i, j: (i, 0)),
                  pl.BlockSpec((tm, pw), lambda i, j: (i, 0)),
                  pl.BlockSpec((POOL_HALO, pw), lambda i, j: (jnp.maximum(i * hb - 1, 0), 0)),
                  pl.BlockSpec((tm, tn), lambda i, j: (i, nu + j)),
                  pl.BlockSpec((tm, tn), lambda i, j: (i, nu + d // tn + j)),
                  pl.BlockSpec((d, tn), lambda i, j: (0, j)),
                  pl.BlockSpec((pw, tn), lambda i, j: (0, j)),
                  pl.BlockSpec(pool_map.shape, lambda i, j: (0, 0, 0)),
                  pl.BlockSpec((1, pw), lambda i, j: (0, 0))],
        out_specs=pl.BlockSpec((tm, tn), lambda i, j: (i, j)),
        out_shape=jax.ShapeDtypeStruct((m, d), BF16),
        scratch_shapes=[pltpu.VMEM((tm + POOL_HALO, pw), F32), pltpu.VMEM((tm, pw), BF16)],
        compiler_params=_cparams("parallel", "arbitrary"),
        name="branch_mix",
    )(att, zc, zc, zc, zc, wa, wp, pool_map, pool_scale.reshape(1, pw))


def _mix_rows_kernel(att_ref, zc_ref, st_ref, wa_ref, wp_ref, pm_ref, ps_ref, o_ref, mixed_ref):
    nbuf, _, pw = st_ref.shape
    d = att_ref.shape[1]
    gw = pw // len(POOL_WINDOWS)
    u = zc_ref[:, 0:pw]
    for g, wdw in enumerate(POOL_WINDOWS):
        cols = slice(g * gw, (g + 1) * gw)
        tot = u[:, cols]
        for r in range(nbuf - (wdw - 1), nbuf):
            tot = tot + st_ref[r, :, cols]
        pooled = tot / float(wdw) - u[:, cols]
        mixed = jnp.dot(pooled.astype(BF16), pm_ref[g], preferred_element_type=F32)
        mixed_ref[:, cols] = (mixed * ps_ref[:, cols]).astype(BF16)
    a = jnp.dot(att_ref[...], wa_ref[...], preferred_element_type=F32)
    b = jnp.dot(mixed_ref[...], wp_ref[...], preferred_element_type=F32)
    ga = zc_ref[:, pw:pw + d]
    gb = zc_ref[:, pw + d:pw + 2 * d]
    o_ref[...] = (jax.nn.sigmoid(ga) * a + jax.nn.sigmoid(gb) * b).astype(BF16)


def _branch_mix_rows(att, zc, state, wa, wp, pool_map, pool_scale):
    m, d = att.shape
    pw = wp.shape[0]
    full = lambda a: pl.BlockSpec(a.shape, lambda i: (0,) * a.ndim)
    ps = pool_scale.reshape(1, pw)
    args = (att, zc, state, wa, wp, pool_map, ps)
    return pl.pallas_call(
        _mix_rows_kernel, grid=(1,), in_specs=[full(a) for a in args],
        out_specs=pl.BlockSpec((m, d), lambda i: (0, 0)),
        out_shape=jax.ShapeDtypeStruct((m, d), BF16),
        scratch_shapes=[pltpu.VMEM((m, pw), BF16)],
        compiler_params=_cparams("arbitrary"),
        name="branch_mix_rows",
    )(*args)


def _outproj_kernel(m_ref, w_ref, x_ref, o_ref):
    o_ref[...] = x_ref[...] + jnp.dot(m_ref[...], w_ref[...], preferred_element_type=F32)


def _out_project(mix, w, x):
    m, d = x.shape
    tm = _tile(m, 1024)
    tn = _tile(d, 1024)
    return pl.pallas_call(
        _outproj_kernel, grid=(m // tm, d // tn),
        in_specs=[pl.BlockSpec((tm, d), lambda i, j: (i, 0)),
                  pl.BlockSpec((d, tn), lambda i, j: (0, j)),
                  pl.BlockSpec((tm, tn), lambda i, j: (i, j))],
        out_specs=pl.BlockSpec((tm, tn), lambda i, j: (i, j)),
        out_shape=jax.ShapeDtypeStruct((m, d), F32),
        compiler_params=_cparams("parallel", "parallel"),
        name="out_project",
    )(mix, w, x)


def _sample_gate_kernel(q_ref, km_ref, o_ref, *, nb):
    gate = jnp.sum(km_ref[0] * q_ref[...], axis=-1, keepdims=True)
    _, picks = _top_blocks(gate, nb, nb, axis=0)
    for t, first in enumerate(picks):
        o_ref[0, t] = first[0]


def _sample_gate(q, kmean):
    b, nb, h, hd = kmean.shape
    n_sel = min(MOBA_TOPK, nb)
    return pl.pallas_call(
        functools.partial(_sample_gate_kernel, nb=nb),
        grid=(b,),
        in_specs=[pl.BlockSpec((1, h, hd), lambda i: (i, 0, 0)),
                  pl.BlockSpec((1, nb, h, hd), lambda i: (i, 0, 0, 0))],
        out_specs=pl.BlockSpec((1, n_sel, h, 1), lambda i: (i, 0, 0, 0)),
        out_shape=jax.ShapeDtypeStruct((b, n_sel, h, 1), jnp.int32),
        compiler_params=_cparams("parallel"),
        name="sample_gate",
    )(q, kmean)


def _paged_attn_kernel(sel_ref, pt_ref, q_ref, kn_ref, vn_ref, ck_ref, cv_ref, o_ref, kbuf, vbuf, sems,
                       *, scale, n_sel, n_pages, ppb):
    b = pl.program_id(0)
    nbatch = pl.num_programs(0)
    h_all, page = ck_ref.shape[2], ck_ref.shape[1]

    def copies(bi, slot):
        out = []
        for h in range(h_all):
            for t in range(n_sel):
                blk = sel_ref[(bi * n_sel + t) * h_all + h]
                for part in range(ppb):
                    pg = pt_ref[bi * n_pages + blk * ppb + part]
                    rows = pl.ds((t * ppb + part) * page, page)
                    out.append(pltpu.make_async_copy(ck_ref.at[pg, :, h, :], kbuf.at[slot, h, rows, :],
                                                     sems.at[slot, 0]))
                    out.append(pltpu.make_async_copy(cv_ref.at[pg, :, h, :], vbuf.at[slot, h, rows, :],
                                                     sems.at[slot, 1]))
        return out

    slot = b % 2

    @pl.when(b == 0)
    def _():
        for c in copies(b, slot):
            c.start()

    @pl.when(b + 1 < nbatch)
    def _():
        for c in copies(b + 1, 1 - slot):
            c.start()

    for c in copies(b, slot):
        c.wait()

    for h in range(h_all):
        q = q_ref[0, h:h + 1, :]
        kn = kn_ref[0, h:h + 1, :]
        vn = vn_ref[0, h:h + 1, :]
        kp = kbuf[slot, h].astype(BF16)
        vp = vbuf[slot, h].astype(BF16)
        q8 = jnp.broadcast_to(q, (8, q.shape[1])).astype(BF16)
        s = lax.dot_general(q8, kp, (((1,), (1,)), ((), ())), preferred_element_type=F32)[0:1] * scale
        s0 = jnp.sum(q * kn, axis=1, keepdims=True) * scale
        m = jnp.maximum(jnp.max(s, axis=1, keepdims=True), s0)
        pr = jnp.exp(s - m)
        p0 = jnp.exp(s0 - m)
        l = jnp.sum(pr, axis=1, keepdims=True) + p0
        pv = jnp.dot(jnp.broadcast_to(pr, (8, pr.shape[1])).astype(BF16), vp, preferred_element_type=F32)[0:1]
        o_ref[0, h:h + 1, :] = (pv + p0 * vn) / l


def _paged_attention(q, k_new, v_new, cache_k, cache_v, sel, page_table):
    b, h, hd = q.shape
    n_sel = sel.shape[1]
    n_pages = page_table.shape[1]
    page = cache_k.shape[1]
    ppb = MOBA_BLOCK // page
    tok = pl.BlockSpec((1, h, hd), lambda bi, s, t: (bi, 0, 0))
    hbm = pl.BlockSpec(memory_space=pl.ANY)
    return pl.pallas_call(
        functools.partial(_paged_attn_kernel, scale=hd ** -0.5, n_sel=n_sel, n_pages=n_pages, ppb=ppb),
        grid_spec=pltpu.PrefetchScalarGridSpec(
            num_scalar_prefetch=2, grid=(b,),
            in_specs=[tok, tok, tok, hbm, hbm],
            out_specs=tok,
            scratch_shapes=[pltpu.VMEM((2, h, n_sel * MOBA_BLOCK, hd), F32),
                            pltpu.VMEM((2, h, n_sel * MOBA_BLOCK, hd), F32),
                            pltpu.SemaphoreType.DMA((2, 2))]),
        out_shape=jax.ShapeDtypeStruct((b, h, hd), F32),
        compiler_params=_cparams("arbitrary"),
        name="paged_attention",
    )(sel.reshape(-1), page_table.reshape(-1), q, k_new, v_new, cache_k, cache_v)


def _layer(x, pos, weights, attend, *, pos0, pool_prefix):
    (g1, f1g, f1u, f1d, gm, w_in, pool_map, pool_scale, wa, wp, wo, g2, f2g, f2u, f2d, gf) = weights
    m, d = x.shape
    hd = d // N_HEADS
    x1, h = _half_ffn(x, g1, f1g, f1u, f1d, gm, final=False)
    tabs = _rotary_tables(pos, hd)
    (q,) = _project(h, w_in, 0, d, "q", tabs)
    want_kmean = pool_prefix is None
    kres = _project(h, w_in, d, d, "k", tabs, want_kmean=want_kmean)
    k, kb = kres[0], kres[1]
    kmean = kres[2].reshape(-1, d) if want_kmean else None
    v, vb = _project(h, w_in, 2 * d, d, "v")
    (zc,) = _project(h, w_in, 3 * d, w_in.shape[1] - 3 * d, "plain")
    att = attend(q, kb, vb, kmean)
    mix = _branch_mix(att, zc, zc if pool_prefix is None else pool_prefix, wa, wp, pool_map, pool_scale,
                      pos0=pos0, halo_is_state=pool_prefix is not None)
    x2 = _out_project(mix, wo, x1)
    y = _half_ffn(x2, g2, f2g, f2u, f2d, gf, final=True)
    return y, k, v, zc[:, :wp.shape[0]]


def kernel(x_prompt, x_sample, cache_k, cache_v, page_table, state_pool, norm_ffn1, ffn1_gate, ffn1_up, ffn1_down,
           norm_mix, w_in, pool_map, pool_scale, w_branch_attn, w_branch_pool, w_out, norm_ffn2, ffn2_gate,
           ffn2_up, ffn2_down, norm_final):
    batch, seq, d = x_prompt.shape
    dec_batch, dec_seq, _ = x_sample.shape
    depth, n_phys, page, _, hd = cache_k.shape
    n_pages = page_table.shape[1]
    past_len = n_pages * page
    pw = pool_scale.shape[1]
    nbuf = state_pool.shape[2]
    assert batch == 1 and dec_seq == 1 and depth == 1 and d == N_HEADS * hd
    assert seq % MOBA_BLOCK == 0 and past_len % MOBA_BLOCK == 0 and nbuf == max(POOL_WINDOWS) - 1
    assert past_len + 1 >= max(POOL_WINDOWS)

    lyr = 0
    wb = _to_bf16
    weights = (norm_ffn1[lyr], wb(ffn1_gate[lyr]), wb(ffn1_up[lyr]), wb(ffn1_down[lyr]), norm_mix[lyr],
               wb(w_in[lyr]), pool_map[lyr].astype(BF16), pool_scale[lyr], wb(w_branch_attn[lyr]),
               wb(w_branch_pool[lyr]), wb(w_out[lyr]),
               norm_ffn2[lyr], wb(ffn2_gate[lyr]), wb(ffn2_up[lyr]), wb(ffn2_down[lyr]), norm_final)

    ck, cv = cache_k[lyr], cache_v[lyr]
    page_means = []

    def attend_prompt(q, kb, vb, kmean):
        att, means = _moba_prompt(q, kb, vb, kmean, ck, page_table)
        page_means.append(means)
        return att

    pos_p = jnp.arange(seq, dtype=jnp.int32)
    y_p, k_p, v_p, u_p = _layer(x_prompt[0], pos_p, weights, attend_prompt, pos0=0, pool_prefix=None)

    def attend_sample(q, kb, vb, _):
        tok = lambda a: a.astype(F32).reshape(dec_batch, N_HEADS, hd)
        sel = _sample_gate(tok(q), page_means[0])
        att = _paged_attention(tok(q), tok(kb), tok(vb), ck, cv, sel[..., 0], page_table)
        return att.reshape(dec_batch, d).astype(BF16)

    pos_s = jnp.full((dec_batch,), past_len, dtype=jnp.int32)
    y_s, k_s, v_s, u_s = _layer(x_sample[:, 0], pos_s, weights, attend_sample, pos0=past_len,
                                pool_prefix=jnp.swapaxes(state_pool[lyr], 0, 1))

    heads = lambda a, b_, l_: a.reshape(1, b_, l_, N_HEADS, hd)
    pool_p = u_p[seq - nbuf:].reshape(1, 1, nbuf, pw)
    pool_s = jnp.concatenate([state_pool[lyr][:, 1:], u_s[:, None, :]], axis=1)[None]
    return (y_p[None], y_s[:, None], heads(k_p, 1, seq), heads(v_p, 1, seq),
            heads(k_s, dec_batch, 1), heads(v_s, dec_batch, 1), pool_p, pool_s)
```

```python
import functools

import jax
import jax.numpy as jnp
from jax import lax
from jax.experimental import pallas as pl
from jax.experimental.pallas import tpu as pltpu

N_HEADS = 16
ROPE_THETA = 500000.0
MOBA_BLOCK = 256
MOBA_TOPK = 3
POOL_WINDOWS = (2, 4, 8, 16)
POOL_HALO = 16
EPS = 1e-6
V7X_VMEM_LIMIT_BYTES = 56 * 1024 * 1024
KV_GROUP = 4
PAGE_LANES = 8
LOG2E = 1.4426950408889634

F32 = jnp.float32
BF16 = jnp.bfloat16
NEG_INF = float("-inf")
POS_INF = float("inf")


def _cparams(*sem):
    return pltpu.CompilerParams(dimension_semantics=sem, vmem_limit_bytes=V7X_VMEM_LIMIT_BYTES)


def _rms(x, g):
    y = x * lax.rsqrt(jnp.mean(x * x, axis=-1, keepdims=True) + EPS)
    return y * g


def _tile(n, pref):
    t = min(n, pref)
    assert n % t == 0, (n, t)
    return t


CAST_BLOCK_BYTES = 6 * 1024 * 1024


def _cast_kernel(w_ref, o_ref):
    o_ref[...] = w_ref[...].astype(BF16)


def _to_bf16(w):
    r, c = w.shape
    tr = r
    while tr * c * 4 > CAST_BLOCK_BYTES and tr % 32 == 0:
        tr //= 2
    return pl.pallas_call(
        _cast_kernel, grid=(r // tr,),
        in_specs=[pl.BlockSpec((tr, c), lambda i: (i, 0))],
        out_specs=pl.BlockSpec((tr, c), lambda i: (i, 0)),
        out_shape=jax.ShapeDtypeStruct((r, c), BF16),
        compiler_params=_cparams("parallel"),
        name="to_bf16",
    )(w)


def _ffn_kernel(x_ref, g_ref, wg_ref, wu_ref, wd_ref, g2_ref, *rest, final):
    if final:
        out_ref, h_ref = rest
    else:
        out_ref, hn_ref, h_ref = rest
    j = pl.program_id(1)

    @pl.when(j == 0)
    def _():
        x = x_ref[...]
        h_ref[...] = _rms(x, g_ref[...]).astype(BF16)
        out_ref[...] = x

    h = h_ref[...]
    gt = jnp.dot(h, wg_ref[...], preferred_element_type=F32)
    up = jnp.dot(h, wu_ref[...], preferred_element_type=F32)
    a = (0.5 * (gt * jax.nn.sigmoid(gt)) * up).astype(BF16)
    out_ref[...] += jnp.dot(a, wd_ref[...], preferred_element_type=F32)

    @pl.when(j == pl.num_programs(1) - 1)
    def _():
        y = out_ref[...]
        if final:
            out_ref[...] = _rms(y, g2_ref[...])
        else:
            hn_ref[...] = _rms(y, g2_ref[...]).astype(BF16)


def _half_ffn(x, g, wg, wu, wd, g2, *, final):
    m, d = x.shape
    ff = wg.shape[1]
    tm = _tile(m, 512)
    tf = _tile(ff, 512)
    row = pl.BlockSpec((tm, d), lambda i, j: (i, 0))
    vec = pl.BlockSpec((1, d), lambda i, j: (0, 0))
    out_shape = [jax.ShapeDtypeStruct((m, d), F32)]
    out_specs = [row]
    if not final:
        out_shape.append(jax.ShapeDtypeStruct((m, d), BF16))
        out_specs.append(row)
    res = pl.pallas_call(
        functools.partial(_ffn_kernel, final=final),
        grid=(m // tm, ff // tf),
        in_specs=[row, vec,
                  pl.BlockSpec((d, tf), lambda i, j: (0, j)),
                  pl.BlockSpec((d, tf), lambda i, j: (0, j)),
                  pl.BlockSpec((tf, d), lambda i, j: (j, 0)),
                  vec],
        out_specs=out_specs,
        out_shape=out_shape,
        scratch_shapes=[pltpu.VMEM((tm, d), BF16)],
        compiler_params=_cparams("parallel", "arbitrary"),
        name="half_ffn_final" if final else "half_ffn",
    )(x, g.reshape(1, d), wg, wu, wd, g2.reshape(1, d))
    return res[0] if final else (res[0], res[1])


def _rotary(z, c, a, b, hd):
    outs = []
    for s in range(z.shape[1] // hd):
        x = z[:, s * hd:(s + 1) * hd]
        left = pltpu.roll(x, hd - 16, axis=1)
        right = pltpu.roll(x, 16, axis=1)
        outs.append(x * c + left * a + right * b)
    return outs


def _proj_kernel(h_ref, w_ref, *rest, mode, hd):
    z = jnp.dot(h_ref[...], w_ref[...], preferred_element_type=F32)
    if mode == "plain":
        (o_ref,) = rest
        o_ref[...] = z
    elif mode == "v":
        o_ref, ob_ref = rest
        o_ref[...] = z
        ob_ref[...] = z.astype(BF16)
    elif mode == "q":
        c_ref, a_ref, b_ref, ob_ref = rest
        for s, r in enumerate(_rotary(z, c_ref[...], a_ref[...], b_ref[...], hd)):
            ob_ref[:, s * hd:(s + 1) * hd] = r.astype(BF16)
    else:
        c_ref, a_ref, b_ref, o_ref, ob_ref, km_ref = rest
        tm = z.shape[0]
        for s, r in enumerate(_rotary(z, c_ref[...], a_ref[...], b_ref[...], hd)):
            o_ref[:, s * hd:(s + 1) * hd] = r
            ob_ref[:, s * hd:(s + 1) * hd] = r.astype(BF16)
            if km_ref is not None:
                nblk = tm // MOBA_BLOCK
                km_ref[0, :, s * hd:(s + 1) * hd] = (
                    r.reshape(nblk, MOBA_BLOCK, hd).sum(axis=1) * (1.0 / MOBA_BLOCK))


def _project(h, w, col0, n, mode, tabs=None, want_kmean=False):
    m, k = h.shape
    hd = k // N_HEADS
    tm = _tile(m, 1024)
    tn = _tile(n, 1024)
    assert col0 % tn == 0
    cb0 = col0 // tn
    grid = (m // tm, n // tn)
    in_specs = [pl.BlockSpec((tm, k), lambda i, j: (i, 0)), pl.BlockSpec((k, tn), lambda i, j: (0, cb0 + j))]
    args = [h, w]
    blk = pl.BlockSpec((tm, tn), lambda i, j: (i, j))
    if mode in ("q", "k"):
        in_specs += [pl.BlockSpec((tm, hd), lambda i, j: (i, 0))] * 3
        args += list(tabs)
    if mode == "plain":
        out_shape, out_specs = [jax.ShapeDtypeStruct((m, n), F32)], [blk]
    elif mode == "v":
        out_shape = [jax.ShapeDtypeStruct((m, n), F32), jax.ShapeDtypeStruct((m, n), BF16)]
        out_specs = [blk, blk]
    elif mode == "q":
        out_shape, out_specs = [jax.ShapeDtypeStruct((m, n), BF16)], [blk]
    else:
        out_shape = [jax.ShapeDtypeStruct((m, n), F32), jax.ShapeDtypeStruct((m, n), BF16)]
        out_specs = [blk, blk]
        if want_kmean:
            nblk = tm // MOBA_BLOCK
            out_shape.append(jax.ShapeDtypeStruct((m // tm, nblk, n), F32))
            out_specs.append(pl.BlockSpec((1, nblk, tn), lambda i, j: (i, 0, j)))
    kern = functools.partial(_proj_kernel, mode=mode, hd=hd)
    if mode == "k" and not want_kmean:
        kern = functools.partial(_proj_kernel_no_kmean, hd=hd)
    return pl.pallas_call(
        kern, grid=grid, in_specs=in_specs, out_specs=out_specs, out_shape=out_shape,
        compiler_params=_cparams("parallel", "parallel"), name="proj_" + mode,
    )(*args)


def _proj_kernel_no_kmean(h_ref, w_ref, c_ref, a_ref, b_ref, o_ref, ob_ref, *, hd):
    _proj_kernel(h_ref, w_ref, c_ref, a_ref, b_ref, o_ref, ob_ref, None, mode="k", hd=hd)


def _rotary_tables(pos, hd):
    rot = hd // 4
    half = rot // 2
    inv = jnp.power(ROPE_THETA, -jnp.arange(0, rot, 2, dtype=F32) / rot)
    ang = pos.astype(F32)[:, None] * inv[None, :]
    cos, sin = jnp.cos(ang), jnp.sin(ang)
    n = pos.shape[0]
    ones = jnp.ones((n, hd - rot), F32)
    zeros = jnp.zeros((n, hd - rot), F32)
    zh = jnp.zeros((n, half), F32)
    c = jnp.concatenate([cos, cos, ones], axis=1)
    a = jnp.concatenate([-sin, zh, zeros], axis=1)
    b = jnp.concatenate([zh, sin, zeros], axis=1)
    return c, a, b


def _top_blocks(gate, limit, nb, axis):
    idx = lax.broadcasted_iota(jnp.int32, gate.shape, axis)
    g = jnp.where(idx < limit, gate, NEG_INF)
    sel = jnp.zeros(gate.shape, F32)
    picks = []
    for _ in range(min(MOBA_TOPK, nb)):
        m = jnp.max(g, axis=axis, keepdims=True)
        cand = jnp.where((g == m) & (m > NEG_INF), idx, nb)
        first = jnp.min(cand, axis=axis, keepdims=True)
        pick = idx == first
        sel = jnp.where(pick, 1.0, sel)
        g = jnp.where(pick, NEG_INF, g)
        picks.append(first)
    return sel, picks


def _moba_kernel(pt_ref, q_ref, k_ref, v_ref, sel_ref, *rest, nb, c, group, n_page_refs, ppb, n_page_steps):
    page_refs = rest[:n_page_refs]
    o_ref, pm_ref, vt_ref, ta_ref, tb_ref = rest[n_page_refs:]
    i = pl.program_id(1)
    bs = q_ref.shape[0]

    def page_means():
        for blk in range(n_page_refs // ppb):
            tot = None
            for r in page_refs[blk * ppb:(blk + 1) * ppb]:
                x = r[0]
                part = x.reshape(PAGE_LANES, x.shape[0] // PAGE_LANES, *x.shape[1:]).sum(axis=1)
                tot = part if tot is None else tot + part
            pm_ref[0, blk] = tot.sum(axis=0) * (1.0 / MOBA_BLOCK)

    @pl.when(i == 0)
    def _():
        for n in range(nb):
            rows = slice(n * bs, (n + 1) * bs)
            vt_ref[:, rows] = v_ref[rows, :].astype(F32).T.astype(BF16)

    qtb = q_ref[...].astype(F32).T.astype(BF16)
    if n_page_steps >= pl.num_programs(0) * nb:
        page_means()
    else:
        pl.when(pl.program_id(0) * nb + i < n_page_steps)(page_means)
    key_ix = lax.broadcasted_iota(jnp.int32, (bs, bs), 0)
    qry_ix = lax.broadcasted_iota(jnp.int32, (bs, bs), 1)
    future = key_ix > qry_ix

    def block_of(trip, g):
        j = trip * group + g
        if g == 0:
            is_own = trip == 0
            return jnp.where(is_own, i, jnp.minimum(j - 1, nb - 1)), is_own
        return jnp.minimum(j - 1, nb - 1), None

    def visible(n, is_own):
        ch = sel_ref[pl.ds(n, 1), :] > 0.0
        return ch if is_own is None else jnp.logical_or(ch, is_own)

    def score_stage(trip, t_buf):
        mxs = []
        for g in range(group):
            n, is_own = block_of(trip, g)
            off = pl.multiple_of(n * bs, bs)
            t = jnp.dot(k_ref[pl.ds(off, bs), :], qtb, preferred_element_type=F32) * c
            if is_own is not None:
                t = jnp.where(jnp.logical_and(future, is_own), NEG_INF, t)
            t_buf[g] = t
            mxs.append(jnp.where(visible(n, is_own), jnp.max(t, axis=0, keepdims=True), NEG_INF))
        return tuple(mxs)

    def soft_stage(trip, t_buf, state, mxs):
        m, l, acc = state
        m_new = m
        for mx in mxs:
            m_new = jnp.maximum(m_new, mx)
        alpha = jnp.exp2(m - m_new)
        l = alpha * l
        acc = alpha * acc
        for g in range(group):
            n, is_own = block_of(trip, g)
            off = pl.multiple_of(n * bs, bs)
            p = jnp.exp2(t_buf[g] - jnp.where(visible(n, is_own), m_new, POS_INF))
            l = l + jnp.sum(p, axis=0, keepdims=True)
            acc = acc + jnp.dot(vt_ref[:, pl.ds(off, bs)], p.astype(BF16), preferred_element_type=F32)
        return m_new, l, acc

    def body(u, carry):
        state, mxs_a = carry
        mxs_b = score_stage(2 * u + 1, tb_ref)
        state = soft_stage(2 * u, ta_ref, state, mxs_a)
        mxs_a = score_stage(2 * u + 2, ta_ref)
        state = soft_stage(2 * u + 1, tb_ref, state, mxs_b)
        return state, mxs_a

    hd = qtb.shape[0]
    init = (jnp.full((1, bs), NEG_INF, F32), jnp.zeros((1, bs), F32), jnp.zeros((hd, bs), F32))
    n_trips = (i + group) // group
    n_pairs = n_trips // 2
    state, mxs_a = lax.fori_loop(0, n_pairs, body, (init, score_stage(0, ta_ref)))
    m, l, acc = lax.cond(n_trips % 2 == 1, lambda: soft_stage(2 * n_pairs, ta_ref, state, mxs_a), lambda: state)
    o_ref[...] = (acc / l).T.astype(BF16)


def _gate_kernel(q_ref, km_ref, o_ref, *, nb):
    j = pl.program_id(1)
    tq = q_ref.shape[0]
    gate = lax.dot_general(km_ref[...], q_ref[...].astype(F32), (((1,), (1,)), ((), ())),
                           precision=lax.Precision.HIGHEST, preferred_element_type=F32)
    own = (j * tq + lax.broadcasted_iota(jnp.int32, gate.shape, 1)) // MOBA_BLOCK
    sel, _ = _top_blocks(gate, own, nb, axis=0)
    o_ref[0] = sel


def _moba_select(q, kmean):
    l, w = q.shape
    hd = w // N_HEADS
    nb = l // MOBA_BLOCK
    tq = _tile(l, 2048)
    return pl.pallas_call(
        functools.partial(_gate_kernel, nb=nb),
        grid=(N_HEADS, l // tq),
        in_specs=[pl.BlockSpec((tq, hd), lambda h, j: (j, h)),
                  pl.BlockSpec((nb, hd), lambda h, j: (0, h))],
        out_specs=pl.BlockSpec((1, nb, tq), lambda h, j: (h, 0, j)),
        out_shape=jax.ShapeDtypeStruct((N_HEADS, nb, l), F32),
        compiler_params=_cparams("parallel", "parallel"),
        name="moba_select",
    )(q, kmean)


def _moba_prompt(q, k, v, kmean, cache, page_table):
    l, w = q.shape
    hd = w // N_HEADS
    nb = l // MOBA_BLOCK
    n_phys, page, h_c, hd_c = cache.shape
    b, n_pages = page_table.shape
    ppb = MOBA_BLOCK // page
    steps = N_HEADS * nb
    total_pages = b * n_pages
    pps = ppb * pl.cdiv(pl.cdiv(total_pages, steps), ppb)
    bps = pps // ppb
    nb_past = n_pages // ppb
    assert MOBA_BLOCK == ppb * page and n_pages % pps == 0
    n_page_steps = total_pages // pps
    assert n_page_steps <= steps
    groups_per_seq = nb_past // bps
    sel = _moba_select(q, kmean)

    def page_spec(part):
        def imap(h, i, pt):
            s = jnp.minimum(h * nb + i, n_page_steps - 1)
            return (pt[s * pps + part], 0, 0, 0)
        return pl.BlockSpec((1, page, h_c, hd_c), imap)

    def mean_map(h, i, pt):
        s = jnp.minimum(h * nb + i, n_page_steps - 1)
        return (s // groups_per_seq, s % groups_per_seq, 0, 0)

    return pl.pallas_call(
        functools.partial(_moba_kernel, nb=nb, c=hd ** -0.5 * LOG2E, group=KV_GROUP, n_page_refs=pps, ppb=ppb,
                          n_page_steps=n_page_steps),
        grid_spec=pltpu.PrefetchScalarGridSpec(
            num_scalar_prefetch=1, grid=(N_HEADS, nb),
            in_specs=[pl.BlockSpec((MOBA_BLOCK, hd), lambda h, i, pt: (i, h)),
                      pl.BlockSpec((l, hd), lambda h, i, pt: (0, h)),
                      pl.BlockSpec((l, hd), lambda h, i, pt: (0, h)),
                      pl.BlockSpec((None, nb, MOBA_BLOCK), lambda h, i, pt: (h, 0, i))]
                     + [page_spec(part) for part in range(pps)],
            out_specs=[pl.BlockSpec((MOBA_BLOCK, hd), lambda h, i, pt: (i, h)),
                       pl.BlockSpec((1, bps, h_c, hd_c), mean_map)],
            scratch_shapes=[pltpu.VMEM((hd, l), BF16),
                            pltpu.VMEM((KV_GROUP, MOBA_BLOCK, MOBA_BLOCK), F32),
                            pltpu.VMEM((KV_GROUP, MOBA_BLOCK, MOBA_BLOCK), F32)]),
        out_shape=[jax.ShapeDtypeStruct((l, w), BF16), jax.ShapeDtypeStruct((b, nb_past, h_c, hd_c), F32)],
        compiler_params=_cparams("arbitrary", "arbitrary"),
        name="moba_prompt",
    )(page_table.reshape(-1), q, k, v, sel, *([cache] * pps))


def _mix_kernel(att_ref, u_ref, halo_ref, ga_ref, gb_ref, wa_ref, wp_ref, pm_ref, ps_ref, o_ref,
                ext_ref, mixed_ref, *, pos0, zero_first_halo):
    i = pl.program_id(0)
    j = pl.program_id(1)
    tm = u_ref.shape[0]
    pw = u_ref.shape[1]
    gw = pw // len(POOL_WINDOWS)

    @pl.when(j == 0)
    def _():
        halo = halo_ref[...]
        if zero_first_halo:
            halo = jnp.where(i == 0, 0.0, halo)
        ext_ref[0:POOL_HALO, :] = halo
        ext_ref[POOL_HALO:, :] = u_ref[...]
        pos = pos0 + i * tm + lax.broadcasted_iota(jnp.int32, (tm, 1), 0)
        for g, wdw in enumerate(POOL_WINDOWS):
            cols = slice(g * gw, (g + 1) * gw)
            tot = ext_ref[POOL_HALO:, cols]
            for d in range(1, wdw):
                tot = tot + ext_ref[POOL_HALO - d:POOL_HALO - d + tm, cols]
            cnt = jnp.minimum(wdw, pos + 1).astype(F32)
            pooled = tot / cnt - ext_ref[POOL_HALO:, cols]
            mixed = jnp.dot(pooled.astype(BF16), pm_ref[g], preferred_element_type=F32)
            mixed_ref[:, cols] = (mixed * ps_ref[:, cols]).astype(BF16)

    a = jnp.dot(att_ref[...], wa_ref[...], preferred_element_type=F32)
    b = jnp.dot(mixed_ref[...], wp_ref[...], preferred_element_type=F32)
    o_ref[...] = (jax.nn.sigmoid(ga_ref[...]) * a + jax.nn.sigmoid(gb_ref[...]) * b).astype(BF16)


def _branch_mix(att, zc, halo_src, wa, wp, pool_map, pool_scale, *, pos0, halo_is_state):
    m, d = att.shape
    pw = wp.shape[0]
    tn = _tile(d, 1024)
    nu = pw // tn if pw >= tn else None
    assert pw % tn == 0
    if halo_is_state:
        return _branch_mix_rows(att, zc, halo_src, wa, wp, pool_map, pool_scale)
    tm = _tile(m, 512)
    hb = tm // POOL_HALO
    return pl.pallas_call(
        functools.partial(_mix_kernel, pos0=pos0, zero_first_halo=True),
        grid=(m // tm, d // tn),
        in_specs=[pl.BlockSpec((tm, d), lambda i, j: (i, 0)),
                  pl.BlockSpec((tm, pw), lambda i, j: (i, 0)),
                  pl.BlockSpec((POOL_HALO, pw), lambda i, j: (jnp.maximum(i * hb - 1, 0), 0)),
                  pl.BlockSpec((tm, tn), lambda i, j: (i, nu + j)),
                  pl.BlockSpec((tm, tn), lambda i, j: (i, nu + d // tn + j)),
                  pl.BlockSpec((d, tn), lambda i, j: (0, j)),
                  pl.BlockSpec((pw, tn), lambda i, j: (0, j)),
                  pl.BlockSpec(pool_map.shape, lambda i, j: (0, 0, 0)),
                  pl.BlockSpec((1, pw), lambda i, j: (0, 0))],
        out_specs=pl.BlockSpec((tm, tn), lambda i, j: (i, j)),
        out_shape=jax.ShapeDtypeStruct((m, d), BF16),
        scratch_shapes=[pltpu.VMEM((tm + POOL_HALO, pw), F32), pltpu.VMEM((tm, pw), BF16)],
        compiler_params=_cparams("parallel", "arbitrary"),
        name="branch_mix",
    )(att, zc, zc, zc, zc, wa, wp, pool_map, pool_scale.reshape(1, pw))


def _mix_rows_kernel(att_ref, zc_ref, st_ref, wa_ref, wp_ref, pm_ref, ps_ref, o_ref, mixed_ref):
    nbuf, _, pw = st_ref.shape
    d = att_ref.shape[1]
    gw = pw // len(POOL_WINDOWS)
    u = zc_ref[:, 0:pw]
    for g, wdw in enumerate(POOL_WINDOWS):
        cols = slice(g * gw, (g + 1) * gw)
        tot = u[:, cols]
        for r in range(nbuf - (wdw - 1), nbuf):
            tot = tot + st_ref[r, :, cols]
        pooled = tot / float(wdw) - u[:, cols]
        mixed = jnp.dot(pooled.astype(BF16), pm_ref[g], preferred_element_type=F32)
        mixed_ref[:, cols] = (mixed * ps_ref[:, cols]).astype(BF16)
    a = jnp.dot(att_ref[...], wa_ref[...], preferred_element_type=F32)
    b = jnp.dot(mixed_ref[...], wp_ref[...], preferred_element_type=F32)
    ga = zc_ref[:, pw:pw + d]
    gb = zc_ref[:, pw + d:pw + 2 * d]
    o_ref[...] = (jax.nn.sigmoid(ga) * a + jax.nn.sigmoid(gb) * b).astype(BF16)


def _branch_mix_rows(att, zc, state, wa, wp, pool_map, pool_scale):
    m, d = att.shape
    pw = wp.shape[0]
    full = lambda a: pl.BlockSpec(a.shape, lambda i: (0,) * a.ndim)
    ps = pool_scale.reshape(1, pw)
    args = (att, zc, state, wa, wp, pool_map, ps)
    return pl.pallas_call(
        _mix_rows_kernel, grid=(1,), in_specs=[full(a) for a in args],
        out_specs=pl.BlockSpec((m, d), lambda i: (0, 0)),
        out_shape=jax.ShapeDtypeStruct((m, d), BF16),
        scratch_shapes=[pltpu.VMEM((m, pw), BF16)],
        compiler_params=_cparams("arbitrary"),
        name="branch_mix_rows",
    )(*args)


def _outproj_kernel(m_ref, w_ref, x_ref, o_ref):
    o_ref[...] = x_ref[...] + jnp.dot(m_ref[...], w_ref[...], preferred_element_type=F32)


def _out_project(mix, w, x):
    m, d = x.shape
    tm = _tile(m, 1024)
    tn = _tile(d, 1024)
    return pl.pallas_call(
        _outproj_kernel, grid=(m // tm, d // tn),
        in_specs=[pl.BlockSpec((tm, d), lambda i, j: (i, 0)),
                  pl.BlockSpec((d, tn), lambda i, j: (0, j)),
                  pl.BlockSpec((tm, tn), lambda i, j: (i, j))],
        out_specs=pl.BlockSpec((tm, tn), lambda i, j: (i, j)),
        out_shape=jax.ShapeDtypeStruct((m, d), F32),
        compiler_params=_cparams("parallel", "parallel"),
        name="out_project",
    )(mix, w, x)


def _sample_gate_kernel(q_ref, km_ref, o_ref, *, nb):
    gate = jnp.sum(km_ref[0] * q_ref[...], axis=-1, keepdims=True)
    _, picks = _top_blocks(gate, nb, nb, axis=0)
    for t, first in enumerate(picks):
        o_ref[0, t] = first[0]


def _sample_gate(q, kmean):
    b, nb, h, hd = kmean.shape
    n_sel = min(MOBA_TOPK, nb)
    return pl.pallas_call(
        functools.partial(_sample_gate_kernel, nb=nb),
        grid=(b,),
        in_specs=[pl.BlockSpec((1, h, hd), lambda i: (i, 0, 0)),
                  pl.BlockSpec((1, nb, h, hd), lambda i: (i, 0, 0, 0))],
        out_specs=pl.BlockSpec((1, n_sel, h, 1), lambda i: (i, 0, 0, 0)),
        out_shape=jax.ShapeDtypeStruct((b, n_sel, h, 1), jnp.int32),
        compiler_params=_cparams("parallel"),
        name="sample_gate",
    )(q, kmean)


def _paged_attn_kernel(sel_ref, pt_ref, q_ref, kn_ref, vn_ref, ck_ref, cv_ref, o_ref, kbuf, vbuf, sems,
                       *, scale, n_sel, n_pages, ppb):
    b = pl.program_id(0)
    nbatch = pl.num_programs(0)
    h_all, page = ck_ref.shape[2], ck_ref.shape[1]

    def copies(bi, slot):
        out = []
        for h in range(h_all):
            for t in range(n_sel):
                blk = sel_ref[(bi * n_sel + t) * h_all + h]
                for part in range(ppb):
                    pg = pt_ref[bi * n_pages + blk * ppb + part]
                    rows = pl.ds((t * ppb + part) * page, page)
                    out.append(pltpu.make_async_copy(ck_ref.at[pg, :, h, :], kbuf.at[slot, h, rows, :],
                                                     sems.at[slot, 0]))
                    out.append(pltpu.make_async_copy(cv_ref.at[pg, :, h, :], vbuf.at[slot, h, rows, :],
                                                     sems.at[slot, 1]))
        return out

    slot = b % 2

    @pl.when(b == 0)
    def _():
        for c in copies(b, slot):
            c.start()

    @pl.when(b + 1 < nbatch)
    def _():
        for c in copies(b + 1, 1 - slot):
            c.start()

    for c in copies(b, slot):
        c.wait()

    for h in range(h_all):
        q = q_ref[0, h:h + 1, :]
        kn = kn_ref[0, h:h + 1, :]
        vn = vn_ref[0, h:h + 1, :]
        qs = q * scale
        s = jnp.sum(kbuf[slot, h] * qs, axis=1, keepdims=True)
        s0 = jnp.sum(qs * kn, axis=1, keepdims=True)
        m = jnp.maximum(jnp.max(s, axis=0, keepdims=True), s0)
        pr = jnp.exp(s - m)
        p0 = jnp.exp(s0 - m)
        l = jnp.sum(pr, axis=0, keepdims=True) + p0
        pv = jnp.sum(pr * vbuf[slot, h], axis=0, keepdims=True)
        o_ref[0, h:h + 1, :] = (pv + p0 * vn) / l


def _paged_attention(q, k_new, v_new, cache_k, cache_v, sel, page_table):
    b, h, hd = q.shape
    n_sel = sel.shape[1]
    n_pages = page_table.shape[1]
    page = cache_k.shape[1]
    ppb = MOBA_BLOCK // page
    tok = pl.BlockSpec((1, h, hd), lambda bi, s, t: (bi, 0, 0))
    hbm = pl.BlockSpec(memory_space=pl.ANY)
    return pl.pallas_call(
        functools.partial(_paged_attn_kernel, scale=hd ** -0.5, n_sel=n_sel, n_pages=n_pages, ppb=ppb),
        grid_spec=pltpu.PrefetchScalarGridSpec(
            num_scalar_prefetch=2, grid=(b,),
            in_specs=[tok, tok, tok, hbm, hbm],
            out_specs=tok,
            scratch_shapes=[pltpu.VMEM((2, h, n_sel * MOBA_BLOCK, hd), F32),
                            pltpu.VMEM((2, h, n_sel * MOBA_BLOCK, hd), F32),
                            pltpu.SemaphoreType.DMA((2, 2))]),
        out_shape=jax.ShapeDtypeStruct((b, h, hd), F32),
        compiler_params=_cparams("arbitrary"),
        name="paged_attention",
    )(sel.reshape(-1), page_table.reshape(-1), q, k_new, v_new, cache_k, cache_v)


def _layer(x, pos, weights, attend, *, pos0, pool_prefix):
    (g1, f1g, f1u, f1d, gm, w_in, pool_map, pool_scale, wa, wp, wo, g2, f2g, f2u, f2d, gf) = weights
    m, d = x.shape
    hd = d // N_HEADS
    x1, h = _half_ffn(x, g1, f1g, f1u, f1d, gm, final=False)
    tabs = _rotary_tables(pos, hd)
    (q,) = _project(h, w_in, 0, d, "q", tabs)
    want_kmean = pool_prefix is None
    kres = _project(h, w_in, d, d, "k", tabs, want_kmean=want_kmean)
    k, kb = kres[0], kres[1]
    kmean = kres[2].reshape(-1, d) if want_kmean else None
    v, vb = _project(h, w_in, 2 * d, d, "v")
    (zc,) = _project(h, w_in, 3 * d, w_in.shape[1] - 3 * d, "plain")
    att = attend(q, kb, vb, kmean)
    mix = _branch_mix(att, zc, zc if pool_prefix is None else pool_prefix, wa, wp, pool_map, pool_scale,
                      pos0=pos0, halo_is_state=pool_prefix is not None)
    x2 = _out_project(mix, wo, x1)
    y = _half_ffn(x2, g2, f2g, f2u, f2d, gf, final=True)
    return y, k, v, zc[:, :wp.shape[0]]


def kernel(x_prompt, x_sample, cache_k, cache_v, page_table, state_pool, norm_ffn1, ffn1_gate, ffn1_up, ffn1_down,
           norm_mix, w_in, pool_map, pool_scale, w_branch_attn, w_branch_pool, w_out, norm_ffn2, ffn2_gate,
           ffn2_up, ffn2_down, norm_final):
    batch, seq, d = x_prompt.shape
    dec_batch, dec_seq, _ = x_sample.shape
    depth, n_phys, page, _, hd = cache_k.shape
    n_pages = page_table.shape[1]
    past_len = n_pages * page
    pw = pool_scale.shape[1]
    nbuf = state_pool.shape[2]
    assert batch == 1 and dec_seq == 1 and depth == 1 and d == N_HEADS * hd
    assert seq % MOBA_BLOCK == 0 and past_len % MOBA_BLOCK == 0 and nbuf == max(POOL_WINDOWS) - 1
    assert past_len + 1 >= max(POOL_WINDOWS)

    lyr = 0
    wb = _to_bf16
    weights = (norm_ffn1[lyr], wb(ffn1_gate[lyr]), wb(ffn1_up[lyr]), wb(ffn1_down[lyr]), norm_mix[lyr],
               wb(w_in[lyr]), pool_map[lyr].astype(BF16), pool_scale[lyr], wb(w_branch_attn[lyr]),
               wb(w_branch_pool[lyr]), wb(w_out[lyr]),
               norm_ffn2[lyr], wb(ffn2_gate[lyr]), wb(ffn2_up[lyr]), wb(ffn2_down[lyr]), norm_final)

    ck, cv = cache_k[lyr], cache_v[lyr]
    page_means = []

    def attend_prompt(q, kb, vb, kmean):
        att, means = _moba_prompt(q, kb, vb, kmean, ck, page_table)
        page_means.append(means)
        return att

    pos_p = jnp.arange(seq, dtype=jnp.int32)
    y_p, k_p, v_p, u_p = _layer(x_prompt[0], pos_p, weights, attend_prompt, pos0=0, pool_prefix=None)

    def attend_sample(q, kb, vb, _):
        tok = lambda a: a.astype(F32).reshape(dec_batch, N_HEADS, hd)
        sel = _sample_gate(tok(q), page_means[0])
        att = _paged_attention(tok(q), tok(kb), tok(vb), ck, cv, sel[..., 0], page_table)
        return att.reshape(dec_batch, d).astype(BF16)

    pos_s = jnp.full((dec_batch,), past_len, dtype=jnp.int32)
    y_s, k_s, v_s, u_s = _layer(x_sample[:, 0], pos_s, weights, attend_sample, pos0=past_len,
                                pool_prefix=jnp.swapaxes(state_pool[lyr], 0, 1))

    heads = lambda a, b_, l_: a.reshape(1, b_, l_, N_HEADS, hd)
    pool_p = u_p[seq - nbuf:].reshape(1, 1, nbuf, pw)
    pool_s = jnp.concatenate([state_pool[lyr][:, 1:], u_s[:, None, :]], axis=1)[None]
    return (y_p[None], y_s[:, None], heads(k_p, 1, seq), heads(v_p, 1, seq),
            heads(k_s, dec_batch, 1), heads(v_s, dec_batch, 1), pool_p, pool_s)
```

```python
import functools

import jax
import jax.numpy as jnp
from jax import lax
from jax.experimental import pallas as pl
from jax.experimental.pallas import tpu as pltpu

N_HEADS = 16
ROPE_THETA = 500000.0
MOBA_BLOCK = 256
MOBA_TOPK = 3
POOL_WINDOWS = (2, 4, 8, 16)
POOL_HALO = 16
EPS = 1e-6
V7X_VMEM_LIMIT_BYTES = 56 * 1024 * 1024
KV_GROUP = 4
PAGE_LANES = 8
Q_SPAN = 2
LOG2E = 1.4426950408889634

F32 = jnp.float32
BF16 = jnp.bfloat16
NEG_INF = float("-inf")
POS_INF = float("inf")


def _cparams(*sem):
    return pltpu.CompilerParams(dimension_semantics=sem, vmem_limit_bytes=V7X_VMEM_LIMIT_BYTES)


def _rms(x, g):
    y = x * lax.rsqrt(jnp.mean(x * x, axis=-1, keepdims=True) + EPS)
    return y * g


def _tile(n, pref):
    t = min(n, pref)
    assert n % t == 0, (n, t)
    return t


CAST_BLOCK_BYTES = 6 * 1024 * 1024


def _cast_kernel(w_ref, o_ref):
    o_ref[...] = w_ref[...].astype(BF16)


def _to_bf16(w):
    r, c = w.shape
    tr = r
    while tr * c * 4 > CAST_BLOCK_BYTES and tr % 32 == 0:
        tr //= 2
    return pl.pallas_call(
        _cast_kernel, grid=(r // tr,),
        in_specs=[pl.BlockSpec((tr, c), lambda i: (i, 0))],
        out_specs=pl.BlockSpec((tr, c), lambda i: (i, 0)),
        out_shape=jax.ShapeDtypeStruct((r, c), BF16),
        compiler_params=_cparams("parallel"),
        name="to_bf16",
    )(w)


def _ffn_kernel(x_ref, g_ref, wg_ref, wu_ref, wd_ref, g2_ref, *rest, final):
    if final:
        out_ref, h_ref = rest
    else:
        out_ref, hn_ref, h_ref = rest
    j = pl.program_id(1)

    @pl.when(j == 0)
    def _():
        x = x_ref[...]
        h_ref[...] = _rms(x, g_ref[...]).astype(BF16)
        out_ref[...] = x

    h = h_ref[...]
    gt = jnp.dot(h, wg_ref[...], preferred_element_type=F32)
    up = jnp.dot(h, wu_ref[...], preferred_element_type=F32)
    a = (0.5 * (gt * jax.nn.sigmoid(gt)) * up).astype(BF16)
    out_ref[...] += jnp.dot(a, wd_ref[...], preferred_element_type=F32)

    @pl.when(j == pl.num_programs(1) - 1)
    def _():
        y = out_ref[...]
        if final:
            out_ref[...] = _rms(y, g2_ref[...])
        else:
            hn_ref[...] = _rms(y, g2_ref[...]).astype(BF16)


def _half_ffn(x, g, wg, wu, wd, g2, *, final):
    m, d = x.shape
    ff = wg.shape[1]
    tm = _tile(m, 512)
    tf = _tile(ff, 512)
    row = pl.BlockSpec((tm, d), lambda i, j: (i, 0))
    vec = pl.BlockSpec((1, d), lambda i, j: (0, 0))
    out_shape = [jax.ShapeDtypeStruct((m, d), F32)]
    out_specs = [row]
    if not final:
        out_shape.append(jax.ShapeDtypeStruct((m, d), BF16))
        out_specs.append(row)
    res = pl.pallas_call(
        functools.partial(_ffn_kernel, final=final),
        grid=(m // tm, ff // tf),
        in_specs=[row, vec,
                  pl.BlockSpec((d, tf), lambda i, j: (0, j)),
                  pl.BlockSpec((d, tf), lambda i, j: (0, j)),
                  pl.BlockSpec((tf, d), lambda i, j: (j, 0)),
                  vec],
        out_specs=out_specs,
        out_shape=out_shape,
        scratch_shapes=[pltpu.VMEM((tm, d), BF16)],
        compiler_params=_cparams("parallel", "arbitrary"),
        name="half_ffn_final" if final else "half_ffn",
    )(x, g.reshape(1, d), wg, wu, wd, g2.reshape(1, d))
    return res[0] if final else (res[0], res[1])


def _rotary(z, c, a, b, hd):
    outs = []
    for s in range(z.shape[1] // hd):
        x = z[:, s * hd:(s + 1) * hd]
        left = pltpu.roll(x, hd - 16, axis=1)
        right = pltpu.roll(x, 16, axis=1)
        outs.append(x * c + left * a + right * b)
    return outs


def _proj_kernel(h_ref, w_ref, *rest, mode, hd):
    z = jnp.dot(h_ref[...], w_ref[...], preferred_element_type=F32)
    if mode == "plain":
        (o_ref,) = rest
        o_ref[...] = z
    elif mode == "v":
        o_ref, ob_ref = rest
        o_ref[...] = z
        ob_ref[...] = z.astype(BF16)
    elif mode == "q":
        c_ref, a_ref, b_ref, ob_ref = rest
        for s, r in enumerate(_rotary(z, c_ref[...], a_ref[...], b_ref[...], hd)):
            ob_ref[:, s * hd:(s + 1) * hd] = r.astype(BF16)
    else:
        c_ref, a_ref, b_ref, o_ref, ob_ref, km_ref = rest
        tm = z.shape[0]
        for s, r in enumerate(_rotary(z, c_ref[...], a_ref[...], b_ref[...], hd)):
            o_ref[:, s * hd:(s + 1) * hd] = r
            ob_ref[:, s * hd:(s + 1) * hd] = r.astype(BF16)
            if km_ref is not None:
                nblk = tm // MOBA_BLOCK
                km_ref[0, :, s * hd:(s + 1) * hd] = (
                    r.reshape(nblk, MOBA_BLOCK, hd).sum(axis=1) * (1.0 / MOBA_BLOCK))


def _project(h, w, col0, n, mode, tabs=None, want_kmean=False):
    m, k = h.shape
    hd = k // N_HEADS
    tm = _tile(m, 1024)
    tn = _tile(n, 1024)
    assert col0 % tn == 0
    cb0 = col0 // tn
    grid = (m // tm, n // tn)
    in_specs = [pl.BlockSpec((tm, k), lambda i, j: (i, 0)), pl.BlockSpec((k, tn), lambda i, j: (0, cb0 + j))]
    args = [h, w]
    blk = pl.BlockSpec((tm, tn), lambda i, j: (i, j))
    if mode in ("q", "k"):
        in_specs += [pl.BlockSpec((tm, hd), lambda i, j: (i, 0))] * 3
        args += list(tabs)
    if mode == "plain":
        out_shape, out_specs = [jax.ShapeDtypeStruct((m, n), F32)], [blk]
    elif mode == "v":
        out_shape = [jax.ShapeDtypeStruct((m, n), F32), jax.ShapeDtypeStruct((m, n), BF16)]
        out_specs = [blk, blk]
    elif mode == "q":
        out_shape, out_specs = [jax.ShapeDtypeStruct((m, n), BF16)], [blk]
    else:
        out_shape = [jax.ShapeDtypeStruct((m, n), F32), jax.ShapeDtypeStruct((m, n), BF16)]
        out_specs = [blk, blk]
        if want_kmean:
            nblk = tm // MOBA_BLOCK
            out_shape.append(jax.ShapeDtypeStruct((m // tm, nblk, n), F32))
            out_specs.append(pl.BlockSpec((1, nblk, tn), lambda i, j: (i, 0, j)))
    kern = functools.partial(_proj_kernel, mode=mode, hd=hd)
    if mode == "k" and not want_kmean:
        kern = functools.partial(_proj_kernel_no_kmean, hd=hd)
    return pl.pallas_call(
        kern, grid=grid, in_specs=in_specs, out_specs=out_specs, out_shape=out_shape,
        compiler_params=_cparams("parallel", "parallel"), name="proj_" + mode,
    )(*args)


def _proj_kernel_no_kmean(h_ref, w_ref, c_ref, a_ref, b_ref, o_ref, ob_ref, *, hd):
    _proj_kernel(h_ref, w_ref, c_ref, a_ref, b_ref, o_ref, ob_ref, None, mode="k", hd=hd)


def _rotary_tables(pos, hd):
    rot = hd // 4
    half = rot // 2
    inv = jnp.power(ROPE_THETA, -jnp.arange(0, rot, 2, dtype=F32) / rot)
    ang = pos.astype(F32)[:, None] * inv[None, :]
    cos, sin = jnp.cos(ang), jnp.sin(ang)
    n = pos.shape[0]
    ones = jnp.ones((n, hd - rot), F32)
    zeros = jnp.zeros((n, hd - rot), F32)
    zh = jnp.zeros((n, half), F32)
    c = jnp.concatenate([cos, cos, ones], axis=1)
    a = jnp.concatenate([-sin, zh, zeros], axis=1)
    b = jnp.concatenate([zh, sin, zeros], axis=1)
    return c, a, b


def _top_blocks(gate, limit, nb, axis):
    idx = lax.broadcasted_iota(jnp.int32, gate.shape, axis)
    g = jnp.where(idx < limit, gate, NEG_INF)
    sel = jnp.zeros(gate.shape, F32)
    picks = []
    for _ in range(min(MOBA_TOPK, nb)):
        m = jnp.max(g, axis=axis, keepdims=True)
        cand = jnp.where((g == m) & (m > NEG_INF), idx, nb)
        first = jnp.min(cand, axis=axis, keepdims=True)
        pick = idx == first
        sel = jnp.where(pick, 1.0, sel)
        g = jnp.where(pick, NEG_INF, g)
        picks.append(first)
    return sel, picks


def _moba_kernel(pt_ref, q_ref, k_ref, v_ref, sel_ref, *rest, nb, c, group, n_page_refs, ppb, n_page_steps):
    page_refs = rest[:n_page_refs]
    o_ref, pm_ref, vt_ref, ta_ref, tb_ref = rest[n_page_refs:]
    i = pl.program_id(1)
    tq = q_ref.shape[0]
    bs = MOBA_BLOCK
    span = tq // bs
    assert group >= span

    def page_means():
        for blk in range(n_page_refs // ppb):
            tot = None
            for r in page_refs[blk * ppb:(blk + 1) * ppb]:
                x = r[0]
                part = x.reshape(PAGE_LANES, x.shape[0] // PAGE_LANES, *x.shape[1:]).sum(axis=1)
                tot = part if tot is None else tot + part
            pm_ref[0, blk] = tot.sum(axis=0) * (1.0 / MOBA_BLOCK)

    @pl.when(i == 0)
    def _():
        for n in range(nb):
            rows = slice(n * bs, (n + 1) * bs)
            vt_ref[:, rows] = v_ref[rows, :].astype(F32).T.astype(BF16)

    qtb = q_ref[...].astype(F32).T.astype(BF16)
    if n_page_steps >= pl.num_programs(0) * pl.num_programs(1):
        page_means()
    else:
        pl.when(pl.program_id(0) * pl.num_programs(1) + i < n_page_steps)(page_means)
    key_ix = lax.broadcasted_iota(jnp.int32, (bs, tq), 0)
    qry_ix = lax.broadcasted_iota(jnp.int32, (bs, tq), 1)
    qry_row = lax.broadcasted_iota(jnp.int32, (1, tq), 1)

    def block_of(trip, g):
        j = trip * group + g - span
        n_past = jnp.clip(j, 0, nb - 1)
        in_past = j < i * span
        if g < span:
            is_own = trip == 0
            return jnp.where(is_own, i * span + g, n_past), is_own, jnp.logical_or(is_own, in_past)
        return n_past, None, in_past

    def visible(n, g, is_own, valid):
        ch = jnp.logical_and(sel_ref[pl.ds(n, 1), :] > 0.0, valid)
        if is_own is None:
            return ch
        return jnp.logical_or(ch, jnp.logical_and(is_own, qry_row // bs == g))

    def score_stage(trip, t_buf):
        mxs = []
        for g in range(group):
            n, is_own, valid = block_of(trip, g)
            off = pl.multiple_of(n * bs, bs)
            t = jnp.dot(k_ref[pl.ds(off, bs), :], qtb, preferred_element_type=F32) * c
            if is_own is not None:
                future = jnp.logical_and(qry_ix // bs == g, key_ix > qry_ix - g * bs)
                t = jnp.where(jnp.logical_and(future, is_own), NEG_INF, t)
            t_buf[g] = t
            mxs.append(jnp.where(visible(n, g, is_own, valid), jnp.max(t, axis=0, keepdims=True), NEG_INF))
        return tuple(mxs)

    def soft_stage(trip, t_buf, state, mxs):
        m, l, acc = state
        m_new = m
        for mx in mxs:
            m_new = jnp.maximum(m_new, mx)
        alpha = jnp.exp2(m - m_new)
        l = alpha * l
        acc = alpha * acc
        for g in range(group):
            n, is_own, valid = block_of(trip, g)
            off = pl.multiple_of(n * bs, bs)
            p = jnp.exp2(t_buf[g] - jnp.where(visible(n, g, is_own, valid), m_new, POS_INF))
            l = l + jnp.sum(p, axis=0, keepdims=True)
            acc = acc + jnp.dot(vt_ref[:, pl.ds(off, bs)], p.astype(BF16), preferred_element_type=F32)
        return m_new, l, acc

    def body(u, carry):
        state, mxs_a = carry
        mxs_b = score_stage(2 * u + 1, tb_ref)
        state = soft_stage(2 * u, ta_ref, state, mxs_a)
        mxs_a = score_stage(2 * u + 2, ta_ref)
        state = soft_stage(2 * u + 1, tb_ref, state, mxs_b)
        return state, mxs_a

    hd = qtb.shape[0]
    init = (jnp.full((1, tq), NEG_INF, F32), jnp.zeros((1, tq), F32), jnp.zeros((hd, tq), F32))
    n_trips = (i * span + span + group - 1) // group
    n_pairs = n_trips // 2
    state, mxs_a = lax.fori_loop(0, n_pairs, body, (init, score_stage(0, ta_ref)))
    m, l, acc = lax.cond(n_trips % 2 == 1, lambda: soft_stage(2 * n_pairs, ta_ref, state, mxs_a), lambda: state)
    o_ref[...] = (acc / l).T.astype(BF16)


def _gate_kernel(q_ref, km_ref, o_ref, *, nb):
    j = pl.program_id(1)
    tq = q_ref.shape[0]
    gate = lax.dot_general(km_ref[...], q_ref[...].astype(F32), (((1,), (1,)), ((), ())),
                           precision=lax.Precision.HIGHEST, preferred_element_type=F32)
    own = (j * tq + lax.broadcasted_iota(jnp.int32, gate.shape, 1)) // MOBA_BLOCK
    sel, _ = _top_blocks(gate, own, nb, axis=0)
    o_ref[0] = sel


def _moba_select(q, kmean):
    l, w = q.shape
    hd = w // N_HEADS
    nb = l // MOBA_BLOCK
    tq = _tile(l, 2048)
    return pl.pallas_call(
        functools.partial(_gate_kernel, nb=nb),
        grid=(N_HEADS, l // tq),
        in_specs=[pl.BlockSpec((tq, hd), lambda h, j: (j, h)),
                  pl.BlockSpec((nb, hd), lambda h, j: (0, h))],
        out_specs=pl.BlockSpec((1, nb, tq), lambda h, j: (h, 0, j)),
        out_shape=jax.ShapeDtypeStruct((N_HEADS, nb, l), F32),
        compiler_params=_cparams("parallel", "parallel"),
        name="moba_select",
    )(q, kmean)


def _moba_prompt(q, k, v, kmean, cache, page_table):
    l, w = q.shape
    hd = w // N_HEADS
    nb = l // MOBA_BLOCK
    n_phys, page, h_c, hd_c = cache.shape
    b, n_pages = page_table.shape
    ppb = MOBA_BLOCK // page
    tq = MOBA_BLOCK * Q_SPAN if nb % Q_SPAN == 0 else MOBA_BLOCK
    nq = l // tq
    steps = N_HEADS * nq
    total_pages = b * n_pages
    pps = ppb * pl.cdiv(pl.cdiv(total_pages, steps), ppb)
    bps = pps // ppb
    nb_past = n_pages // ppb
    assert MOBA_BLOCK == ppb * page and n_pages % pps == 0
    n_page_steps = total_pages // pps
    assert n_page_steps <= steps
    groups_per_seq = nb_past // bps
    sel = _moba_select(q, kmean)

    def page_spec(part):
        def imap(h, i, pt):
            s = jnp.minimum(h * nq + i, n_page_steps - 1)
            return (pt[s * pps + part], 0, 0, 0)
        return pl.BlockSpec((1, page, h_c, hd_c), imap)

    def mean_map(h, i, pt):
        s = jnp.minimum(h * nq + i, n_page_steps - 1)
        return (s // groups_per_seq, s % groups_per_seq, 0, 0)

    return pl.pallas_call(
        functools.partial(_moba_kernel, nb=nb, c=hd ** -0.5 * LOG2E, group=KV_GROUP, n_page_refs=pps, ppb=ppb,
                          n_page_steps=n_page_steps),
        grid_spec=pltpu.PrefetchScalarGridSpec(
            num_scalar_prefetch=1, grid=(N_HEADS, nq),
            in_specs=[pl.BlockSpec((tq, hd), lambda h, i, pt: (i, h)),
                      pl.BlockSpec((l, hd), lambda h, i, pt: (0, h)),
                      pl.BlockSpec((l, hd), lambda h, i, pt: (0, h)),
                      pl.BlockSpec((None, nb, tq), lambda h, i, pt: (h, 0, i))]
                     + [page_spec(part) for part in range(pps)],
            out_specs=[pl.BlockSpec((tq, hd), lambda h, i, pt: (i, h)),
                       pl.BlockSpec((1, bps, h_c, hd_c), mean_map)],
            scratch_shapes=[pltpu.VMEM((hd, l), BF16),
                            pltpu.VMEM((KV_GROUP, MOBA_BLOCK, tq), F32),
                            pltpu.VMEM((KV_GROUP, MOBA_BLOCK, tq), F32)]),
        out_shape=[jax.ShapeDtypeStruct((l, w), BF16), jax.ShapeDtypeStruct((b, nb_past, h_c, hd_c), F32)],
        compiler_params=_cparams("arbitrary", "arbitrary"),
        name="moba_prompt",
    )(page_table.reshape(-1), q, k, v, sel, *([cache] * pps))


def _mix_kernel(att_ref, u_ref, halo_ref, ga_ref, gb_ref, wa_ref, wp_ref, pm_ref, ps_ref, o_ref,
                ext_ref, mixed_ref, *, pos0, zero_first_halo):
    i = pl.program_id(0)
    j = pl.program_id(1)
    tm = u_ref.shape[0]
    pw = u_ref.shape[1]
    gw = pw // len(POOL_WINDOWS)

    @pl.when(j == 0)
    def _():
        halo = halo_ref[...]
        if zero_first_halo:
            halo = jnp.where(i == 0, 0.0, halo)
        ext_ref[0:POOL_HALO, :] = halo
        ext_ref[POOL_HALO:, :] = u_ref[...]
        pos = pos0 + i * tm + lax.broadcasted_iota(jnp.int32, (tm, 1), 0)
        for g, wdw in enumerate(POOL_WINDOWS):
            cols = slice(g * gw, (g + 1) * gw)
            tot = ext_ref[POOL_HALO:, cols]
            for d in range(1, wdw):
                tot = tot + ext_ref[POOL_HALO - d:POOL_HALO - d + tm, cols]
            cnt = jnp.minimum(wdw, pos + 1).astype(F32)
            pooled = tot / cnt - ext_ref[POOL_HALO:, cols]
            mixed = jnp.dot(pooled.astype(BF16), pm_ref[g], preferred_element_type=F32)
            mixed_ref[:, cols] = (mixed * ps_ref[:, cols]).astype(BF16)

    a = jnp.dot(att_ref[...], wa_ref[...], preferred_element_type=F32)
    b = jnp.dot(mixed_ref[...], wp_ref[...], preferred_element_type=F32)
    o_ref[...] = (jax.nn.sigmoid(ga_ref[...]) * a + jax.nn.sigmoid(gb_ref[...]) * b).astype(BF16)


def _branch_mix(att, zc, halo_src, wa, wp, pool_map, pool_scale, *, pos0, halo_is_state):
    m, d = att.shape
    pw = wp.shape[0]
    tn = _tile(d, 1024)
    nu = pw // tn if pw >= tn else None
    assert pw % tn == 0
    if halo_is_state:
        return _branch_mix_rows(att, zc, halo_src, wa, wp, pool_map, pool_scale)
    tm = _tile(m, 512)
    hb = tm // POOL_HALO
    return pl.pallas_call(
        functools.partial(_mix_kernel, pos0=pos0, zero_first_halo=True),
        grid=(m // tm, d // tn),
        in_specs=[pl.BlockSpec((tm, d), lambda i, j: (i, 0)),
                  pl.BlockSpec((tm, pw), lambda i, j: (i, 0)),
                  pl.BlockSpec((POOL_HALO, pw), lambda i, j: (jnp.maximum(i * hb - 1, 0), 0)),
                  pl.BlockSpec((tm, tn), lambda i, j: (i, nu + j)),
                  pl.BlockSpec((tm, tn), lambda i, j: (i, nu + d // tn + j)),
                  pl.BlockSpec((d, tn), lambda i, j: (0, j)),
                  pl.BlockSpec((pw, tn), lambda i, j: (0, j)),
                  pl.BlockSpec(pool_map.shape, lambda i, j: (0, 0, 0)),
                  pl.BlockSpec((1, pw), lambda i, j: (0, 0))],
        out_specs=pl.BlockSpec((tm, tn), lambda i, j: (i, j)),
        out_shape=jax.ShapeDtypeStruct((m, d), BF16),
        scratch_shapes=[pltpu.VMEM((tm + POOL_HALO, pw), F32), pltpu.VMEM((tm, pw), BF16)],
        compiler_params=_cparams("parallel", "arbitrary"),
        name="branch_mix",
    )(att, zc, zc, zc, zc, wa, wp, pool_map, pool_scale.reshape(1, pw))


def _mix_rows_kernel(att_ref, zc_ref, st_ref, wa_ref, wp_ref, pm_ref, ps_ref, o_ref, mixed_ref):
    nbuf, _, pw = st_ref.shape
    d = att_ref.shape[1]
    gw = pw // len(POOL_WINDOWS)
    u = zc_ref[:, 0:pw]
    for g, wdw in enumerate(POOL_WINDOWS):
        cols = slice(g * gw, (g + 1) * gw)
        tot = u[:, cols]
        for r in range(nbuf - (wdw - 1), nbuf):
            tot = tot + st_ref[r, :, cols]
        pooled = tot / float(wdw) - u[:, cols]
        mixed = jnp.dot(pooled.astype(BF16), pm_ref[g], preferred_element_type=F32)
        mixed_ref[:, cols] = (mixed * ps_ref[:, cols]).astype(BF16)
    a = jnp.dot(att_ref[...], wa_ref[...], preferred_element_type=F32)
    b = jnp.dot(mixed_ref[...], wp_ref[...], preferred_element_type=F32)
    ga = zc_ref[:, pw:pw + d]
    gb = zc_ref[:, pw + d:pw + 2 * d]
    o_ref[...] = (jax.nn.sigmoid(ga) * a + jax.nn.sigmoid(gb) * b).astype(BF16)


def _branch_mix_rows(att, zc, state, wa, wp, pool_map, pool_scale):
    m, d = att.shape
    pw = wp.shape[0]
    full = lambda a: pl.BlockSpec(a.shape, lambda i: (0,) * a.ndim)
    ps = pool_scale.reshape(1, pw)
    args = (att, zc, state, wa, wp, pool_map, ps)
    return pl.pallas_call(
        _mix_rows_kernel, grid=(1,), in_specs=[full(a) for a in args],
        out_specs=pl.BlockSpec((m, d), lambda i: (0, 0)),
        out_shape=jax.ShapeDtypeStruct((m, d), BF16),
        scratch_shapes=[pltpu.VMEM((m, pw), BF16)],
        compiler_params=_cparams("arbitrary"),
        name="branch_mix_rows",
    )(*args)


def _outproj_kernel(m_ref, w_ref, x_ref, o_ref):
    o_ref[...] = x_ref[...] + jnp.dot(m_ref[...], w_ref[...], preferred_element_type=F32)


def _out_project(mix, w, x):
    m, d = x.shape
    tm = _tile(m, 1024)
    tn = _tile(d, 1024)
    return pl.pallas_call(
        _outproj_kernel, grid=(m // tm, d // tn),
        in_specs=[pl.BlockSpec((tm, d), lambda i, j: (i, 0)),
                  pl.BlockSpec((d, tn), lambda i, j: (0, j)),
                  pl.BlockSpec((tm, tn), lambda i, j: (i, j))],
        out_specs=pl.BlockSpec((tm, tn), lambda i, j: (i, j)),
        out_shape=jax.ShapeDtypeStruct((m, d), F32),
        compiler_params=_cparams("parallel", "parallel"),
        name="out_project",
    )(mix, w, x)


def _sample_gate_kernel(q_ref, km_ref, o_ref, *, nb):
    gate = jnp.sum(km_ref[0] * q_ref[...], axis=-1, keepdims=True)
    _, picks = _top_blocks(gate, nb, nb, axis=0)
    for t, first in enumerate(picks):
        o_ref[0, t] = first[0]


def _sample_gate(q, kmean):
    b, nb, h, hd = kmean.shape
    n_sel = min(MOBA_TOPK, nb)
    return pl.pallas_call(
        functools.partial(_sample_gate_kernel, nb=nb),
        grid=(b,),
        in_specs=[pl.BlockSpec((1, h, hd), lambda i: (i, 0, 0)),
                  pl.BlockSpec((1, nb, h, hd), lambda i: (i, 0, 0, 0))],
        out_specs=pl.BlockSpec((1, n_sel, h, 1), lambda i: (i, 0, 0, 0)),
        out_shape=jax.ShapeDtypeStruct((b, n_sel, h, 1), jnp.int32),
        compiler_params=_cparams("parallel"),
        name="sample_gate",
    )(q, kmean)


def _paged_attn_kernel(sel_ref, pt_ref, q_ref, kn_ref, vn_ref, ck_ref, cv_ref, o_ref, kbuf, vbuf, sems,
                       *, scale, n_sel, n_pages, ppb):
    b = pl.program_id(0)
    nbatch = pl.num_programs(0)
    h_all, page = ck_ref.shape[2], ck_ref.shape[1]

    def copies(bi, slot):
        out = []
        for h in range(h_all):
            for t in range(n_sel):
                blk = sel_ref[(bi * n_sel + t) * h_all + h]
                for part in range(ppb):
                    pg = pt_ref[bi * n_pages + blk * ppb + part]
                    rows = pl.ds((t * ppb + part) * page, page)
                    out.append(pltpu.make_async_copy(ck_ref.at[pg, :, h, :], kbuf.at[slot, h, rows, :],
                                                     sems.at[slot, 0]))
                    out.append(pltpu.make_async_copy(cv_ref.at[pg, :, h, :], vbuf.at[slot, h, rows, :],
                                                     sems.at[slot, 1]))
        return out

    slot = b % 2

    @pl.when(b == 0)
    def _():
        for c in copies(b, slot):
            c.start()

    @pl.when(b + 1 < nbatch)
    def _():
        for c in copies(b + 1, 1 - slot):
            c.start()

    for c in copies(b, slot):
        c.wait()

    for h in range(h_all):
        q = q_ref[0, h:h + 1, :]
        kn = kn_ref[0, h:h + 1, :]
        vn = vn_ref[0, h:h + 1, :]
        qs = q * scale
        s = jnp.sum(kbuf[slot, h] * qs, axis=1, keepdims=True)
        s0 = jnp.sum(qs * kn, axis=1, keepdims=True)
        m = jnp.maximum(jnp.max(s, axis=0, keepdims=True), s0)
        pr = jnp.exp(s - m)
        p0 = jnp.exp(s0 - m)
        l = jnp.sum(pr, axis=0, keepdims=True) + p0
        pv = jnp.sum(pr * vbuf[slot, h], axis=0, keepdims=True)
        o_ref[0, h:h + 1, :] = (pv + p0 * vn) / l


def _paged_attention(q, k_new, v_new, cache_k, cache_v, sel, page_table):
    b, h, hd = q.shape
    n_sel = sel.shape[1]
    n_pages = page_table.shape[1]
    page = cache_k.shape[1]
    ppb = MOBA_BLOCK // page
    tok = pl.BlockSpec((1, h, hd), lambda bi, s, t: (bi, 0, 0))
    hbm = pl.BlockSpec(memory_space=pl.ANY)
    return pl.pallas_call(
        functools.partial(_paged_attn_kernel, scale=hd ** -0.5, n_sel=n_sel, n_pages=n_pages, ppb=ppb),
        grid_spec=pltpu.PrefetchScalarGridSpec(
            num_scalar_prefetch=2, grid=(b,),
            in_specs=[tok, tok, tok, hbm, hbm],
            out_specs=tok,
            scratch_shapes=[pltpu.VMEM((2, h, n_sel * MOBA_BLOCK, hd), F32),
                            pltpu.VMEM((2, h, n_sel * MOBA_BLOCK, hd), F32),
                            pltpu.SemaphoreType.DMA((2, 2))]),
        out_shape=jax.ShapeDtypeStruct((b, h, hd), F32),
        compiler_params=_cparams("arbitrary"),
        name="paged_attention",
    )(sel.reshape(-1), page_table.reshape(-1), q, k_new, v_new, cache_k, cache_v)


def _layer(x, pos, weights, attend, *, pos0, pool_prefix):
    (g1, f1g, f1u, f1d, gm, w_in, pool_map, pool_scale, wa, wp, wo, g2, f2g, f2u, f2d, gf) = weights
    m, d = x.shape
    hd = d // N_HEADS
    x1, h = _half_ffn(x, g1, f1g, f1u, f1d, gm, final=False)
    tabs = _rotary_tables(pos, hd)
    (q,) = _project(h, w_in, 0, d, "q", tabs)
    want_kmean = pool_prefix is None
    kres = _project(h, w_in, d, d, "k", tabs, want_kmean=want_kmean)
    k, kb = kres[0], kres[1]
    kmean = kres[2].reshape(-1, d) if want_kmean else None
    v, vb = _project(h, w_in, 2 * d, d, "v")
    (zc,) = _project(h, w_in, 3 * d, w_in.shape[1] - 3 * d, "plain")
    att = attend(q, kb, vb, kmean)
    mix = _branch_mix(att, zc, zc if pool_prefix is None else pool_prefix, wa, wp, pool_map, pool_scale,
                      pos0=pos0, halo_is_state=pool_prefix is not None)
    x2 = _out_project(mix, wo, x1)
    y = _half_ffn(x2, g2, f2g, f2u, f2d, gf, final=True)
    return y, k, v, zc[:, :wp.shape[0]]


def kernel(x_prompt, x_sample, cache_k, cache_v, page_table, state_pool, norm_ffn1, ffn1_gate, ffn1_up, ffn1_down,
           norm_mix, w_in, pool_map, pool_scale, w_branch_attn, w_branch_pool, w_out, norm_ffn2, ffn2_gate,
           ffn2_up, ffn2_down, norm_final):
    batch, seq, d = x_prompt.shape
    dec_batch, dec_seq, _ = x_sample.shape
    depth, n_phys, page, _, hd = cache_k.shape
    n_pages = page_table.shape[1]
    past_len = n_pages * page
    pw = pool_scale.shape[1]
    nbuf = state_pool.shape[2]
    assert batch == 1 and dec_seq == 1 and depth == 1 and d == N_HEADS * hd
    assert seq % MOBA_BLOCK == 0 and past_len % MOBA_BLOCK == 0 and nbuf == max(POOL_WINDOWS) - 1
    assert past_len + 1 >= max(POOL_WINDOWS)

    lyr = 0
    wb = _to_bf16
    weights = (norm_ffn1[lyr], wb(ffn1_gate[lyr]), wb(ffn1_up[lyr]), wb(ffn1_down[lyr]), norm_mix[lyr],
               wb(w_in[lyr]), pool_map[lyr].astype(BF16), pool_scale[lyr], wb(w_branch_attn[lyr]),
               wb(w_branch_pool[lyr]), wb(w_out[lyr]),
               norm_ffn2[lyr], wb(ffn2_gate[lyr]), wb(ffn2_up[lyr]), wb(ffn2_down[lyr]), norm_final)

    ck, cv = cache_k[lyr], cache_v[lyr]
    page_means = []

    def attend_prompt(q, kb, vb, kmean):
        att, means = _moba_prompt(q, kb, vb, kmean, ck, page_table)
        page_means.append(means)
        return att

    pos_p = jnp.arange(seq, dtype=jnp.int32)
    y_p, k_p, v_p, u_p = _layer(x_prompt[0], pos_p, weights, attend_prompt, pos0=0, pool_prefix=None)

    def attend_sample(q, kb, vb, _):
        tok = lambda a: a.astype(F32).reshape(dec_batch, N_HEADS, hd)
        sel = _sample_gate(tok(q), page_means[0])
        att = _paged_attention(tok(q), tok(kb), tok(vb), ck, cv, sel[..., 0], page_table)
        return att.reshape(dec_batch, d).astype(BF16)

    pos_s = jnp.full((dec_batch,), past_len, dtype=jnp.int32)
    y_s, k_s, v_s, u_s = _layer(x_sample[:, 0], pos_s, weights, attend_sample, pos0=past_len,
                                pool_prefix=jnp.swapaxes(state_pool[lyr], 0, 1))

    heads = lambda a, b_, l_: a.reshape(1, b_, l_, N_HEADS, hd)
    pool_p = u_p[seq - nbuf:].reshape(1, 1, nbuf, pw)
    pool_s = jnp.concatenate([state_pool[lyr][:, 1:], u_s[:, None, :]], axis=1)[None]
    return (y_p[None], y_s[:, None], heads(k_p, 1, seq), heads(v_p, 1, seq),
            heads(k_s, dec_batch, 1), heads(v_s, dec_batch, 1), pool_p, pool_s)
```

```python
import functools

import jax
import jax.numpy as jnp
from jax import lax
from jax.experimental import pallas as pl
from jax.experimental.pallas import tpu as pltpu

N_HEADS = 16
ROPE_THETA = 500000.0
MOBA_BLOCK = 256
MOBA_TOPK = 3
POOL_WINDOWS = (2, 4, 8, 16)
POOL_HALO = 16
EPS = 1e-6
V7X_VMEM_LIMIT_BYTES = 56 * 1024 * 1024
KV_GROUP = 4
PAGE_LANES = 8
Q_SPAN = 2
LOG2E = 1.4426950408889634

F32 = jnp.float32
BF16 = jnp.bfloat16
NEG_INF = float("-inf")
POS_INF = float("inf")


def _cparams(*sem):
    return pltpu.CompilerParams(dimension_semantics=sem, vmem_limit_bytes=V7X_VMEM_LIMIT_BYTES)


def _rms(x, g):
    y = x * lax.rsqrt(jnp.mean(x * x, axis=-1, keepdims=True) + EPS)
    return y * g


def _tile(n, pref):
    t = min(n, pref)
    assert n % t == 0, (n, t)
    return t


CAST_BLOCK_BYTES = 6 * 1024 * 1024


def _cast_kernel(w_ref, o_ref):
    o_ref[...] = w_ref[...].astype(BF16)


def _to_bf16(w):
    r, c = w.shape
    tr = r
    while tr * c * 4 > CAST_BLOCK_BYTES and tr % 32 == 0:
        tr //= 2
    return pl.pallas_call(
        _cast_kernel, grid=(r // tr,),
        in_specs=[pl.BlockSpec((tr, c), lambda i: (i, 0))],
        out_specs=pl.BlockSpec((tr, c), lambda i: (i, 0)),
        out_shape=jax.ShapeDtypeStruct((r, c), BF16),
        compiler_params=_cparams("parallel"),
        name="to_bf16",
    )(w)


def _ffn_kernel(x_ref, g_ref, wg_ref, wu_ref, wd_ref, g2_ref, *rest, final):
    if final:
        out_ref, h_ref = rest
    else:
        out_ref, hn_ref, h_ref = rest
    j = pl.program_id(1)

    @pl.when(j == 0)
    def _():
        x = x_ref[...]
        h_ref[...] = _rms(x, g_ref[...]).astype(BF16)
        out_ref[...] = x

    h = h_ref[...]
    gt = jnp.dot(h, wg_ref[...], preferred_element_type=F32)
    up = jnp.dot(h, wu_ref[...], preferred_element_type=F32)
    a = (0.5 * (gt * jax.nn.sigmoid(gt)) * up).astype(BF16)
    out_ref[...] += jnp.dot(a, wd_ref[...], preferred_element_type=F32)

    @pl.when(j == pl.num_programs(1) - 1)
    def _():
        y = out_ref[...]
        if final:
            out_ref[...] = _rms(y, g2_ref[...])
        else:
            hn_ref[...] = _rms(y, g2_ref[...]).astype(BF16)


def _half_ffn(x, g, wg, wu, wd, g2, *, final):
    m, d = x.shape
    ff = wg.shape[1]
    tm = _tile(m, 512)
    tf = _tile(ff, 512)
    row = pl.BlockSpec((tm, d), lambda i, j: (i, 0))
    vec = pl.BlockSpec((1, d), lambda i, j: (0, 0))
    out_shape = [jax.ShapeDtypeStruct((m, d), F32)]
    out_specs = [row]
    if not final:
        out_shape.append(jax.ShapeDtypeStruct((m, d), BF16))
        out_specs.append(row)
    res = pl.pallas_call(
        functools.partial(_ffn_kernel, final=final),
        grid=(m // tm, ff // tf),
        in_specs=[row, vec,
                  pl.BlockSpec((d, tf), lambda i, j: (0, j)),
                  pl.BlockSpec((d, tf), lambda i, j: (0, j)),
                  pl.BlockSpec((tf, d), lambda i, j: (j, 0)),
                  vec],
        out_specs=out_specs,
        out_shape=out_shape,
        scratch_shapes=[pltpu.VMEM((tm, d), BF16)],
        compiler_params=_cparams("parallel", "arbitrary"),
        name="half_ffn_final" if final else "half_ffn",
    )(x, g.reshape(1, d), wg, wu, wd, g2.reshape(1, d))
    return res[0] if final else (res[0], res[1])


def _rotary(z, c, a, b, hd):
    outs = []
    for s in range(z.shape[1] // hd):
        x = z[:, s * hd:(s + 1) * hd]
        left = pltpu.roll(x, hd - 16, axis=1)
        right = pltpu.roll(x, 16, axis=1)
        outs.append(x * c + left * a + right * b)
    return outs


def _proj_kernel(h_ref, w_ref, *rest, mode, hd):
    z = jnp.dot(h_ref[...], w_ref[...], preferred_element_type=F32)
    if mode == "plain":
        (o_ref,) = rest
        o_ref[...] = z
    elif mode == "v":
        o_ref, ob_ref = rest
        o_ref[...] = z
        ob_ref[...] = z.astype(BF16)
    elif mode == "q":
        c_ref, a_ref, b_ref, ob_ref = rest
        for s, r in enumerate(_rotary(z, c_ref[...], a_ref[...], b_ref[...], hd)):
            ob_ref[:, s * hd:(s + 1) * hd] = r.astype(BF16)
    else:
        c_ref, a_ref, b_ref, o_ref, ob_ref, km_ref = rest
        tm = z.shape[0]
        for s, r in enumerate(_rotary(z, c_ref[...], a_ref[...], b_ref[...], hd)):
            o_ref[:, s * hd:(s + 1) * hd] = r
            ob_ref[:, s * hd:(s + 1) * hd] = r.astype(BF16)
            if km_ref is not None:
                nblk = tm // MOBA_BLOCK
                km_ref[0, :, s * hd:(s + 1) * hd] = (
                    r.reshape(nblk, MOBA_BLOCK, hd).sum(axis=1) * (1.0 / MOBA_BLOCK))


def _project(h, w, col0, n, mode, tabs=None, want_kmean=False):
    m, k = h.shape
    hd = k // N_HEADS
    if mode == "plain":
        tm, tn = _tile(m, 1024), _tile(n, 1024)
    else:
        tm, tn = _tile(m, 512), n
    assert col0 % tn == 0
    cb0 = col0 // tn
    grid = (m // tm, n // tn)
    in_specs = [pl.BlockSpec((tm, k), lambda i, j: (i, 0)), pl.BlockSpec((k, tn), lambda i, j: (0, cb0 + j))]
    args = [h, w]
    blk = pl.BlockSpec((tm, tn), lambda i, j: (i, j))
    if mode in ("q", "k"):
        in_specs += [pl.BlockSpec((tm, hd), lambda i, j: (i, 0))] * 3
        args += list(tabs)
    if mode == "plain":
        out_shape, out_specs = [jax.ShapeDtypeStruct((m, n), F32)], [blk]
    elif mode == "v":
        out_shape = [jax.ShapeDtypeStruct((m, n), F32), jax.ShapeDtypeStruct((m, n), BF16)]
        out_specs = [blk, blk]
    elif mode == "q":
        out_shape, out_specs = [jax.ShapeDtypeStruct((m, n), BF16)], [blk]
    else:
        out_shape = [jax.ShapeDtypeStruct((m, n), F32), jax.ShapeDtypeStruct((m, n), BF16)]
        out_specs = [blk, blk]
        if want_kmean:
            nblk = tm // MOBA_BLOCK
            out_shape.append(jax.ShapeDtypeStruct((m // tm, nblk, n), F32))
            out_specs.append(pl.BlockSpec((1, nblk, tn), lambda i, j: (i, 0, j)))
    kern = functools.partial(_proj_kernel, mode=mode, hd=hd)
    if mode == "k" and not want_kmean:
        kern = functools.partial(_proj_kernel_no_kmean, hd=hd)
    return pl.pallas_call(
        kern, grid=grid, in_specs=in_specs, out_specs=out_specs, out_shape=out_shape,
        compiler_params=_cparams("parallel", "parallel"), name="proj_" + mode,
    )(*args)


def _proj_kernel_no_kmean(h_ref, w_ref, c_ref, a_ref, b_ref, o_ref, ob_ref, *, hd):
    _proj_kernel(h_ref, w_ref, c_ref, a_ref, b_ref, o_ref, ob_ref, None, mode="k", hd=hd)


def _rotary_tables(pos, hd):
    rot = hd // 4
    half = rot // 2
    inv = jnp.power(ROPE_THETA, -jnp.arange(0, rot, 2, dtype=F32) / rot)
    ang = pos.astype(F32)[:, None] * inv[None, :]
    cos, sin = jnp.cos(ang), jnp.sin(ang)
    n = pos.shape[0]
    ones = jnp.ones((n, hd - rot), F32)
    zeros = jnp.zeros((n, hd - rot), F32)
    zh = jnp.zeros((n, half), F32)
    c = jnp.concatenate([cos, cos, ones], axis=1)
    a = jnp.concatenate([-sin, zh, zeros], axis=1)
    b = jnp.concatenate([zh, sin, zeros], axis=1)
    return c, a, b


def _top_blocks(gate, limit, nb, axis):
    idx = lax.broadcasted_iota(jnp.int32, gate.shape, axis)
    g = jnp.where(idx < limit, gate, NEG_INF)
    sel = jnp.zeros(gate.shape, F32)
    picks = []
    for _ in range(min(MOBA_TOPK, nb)):
        m = jnp.max(g, axis=axis, keepdims=True)
        cand = jnp.where((g == m) & (m > NEG_INF), idx, nb)
        first = jnp.min(cand, axis=axis, keepdims=True)
        pick = idx == first
        sel = jnp.where(pick, 1.0, sel)
        g = jnp.where(pick, NEG_INF, g)
        picks.append(first)
    return sel, picks


def _moba_kernel(pt_ref, q_ref, k_ref, v_ref, sel_ref, *rest, nb, c, group, n_page_refs, ppb, n_page_steps):
    page_refs = rest[:n_page_refs]
    o_ref, pm_ref, vt_ref, ta_ref, tb_ref = rest[n_page_refs:]
    i = pl.program_id(1)
    tq = q_ref.shape[0]
    bs = MOBA_BLOCK
    span = tq // bs
    assert group >= span

    def page_means():
        for blk in range(n_page_refs // ppb):
            tot = None
            for r in page_refs[blk * ppb:(blk + 1) * ppb]:
                x = r[0]
                part = x.reshape(PAGE_LANES, x.shape[0] // PAGE_LANES, *x.shape[1:]).sum(axis=1)
                tot = part if tot is None else tot + part
            pm_ref[0, blk] = tot.sum(axis=0) * (1.0 / MOBA_BLOCK)

    @pl.when(i == 0)
    def _():
        for n in range(nb):
            rows = slice(n * bs, (n + 1) * bs)
            vt_ref[:, rows] = v_ref[rows, :].astype(F32).T.astype(BF16)

    qtb = q_ref[...].astype(F32).T.astype(BF16)
    if n_page_steps >= pl.num_programs(0) * pl.num_programs(1):
        page_means()
    else:
        pl.when(pl.program_id(0) * pl.num_programs(1) + i < n_page_steps)(page_means)
    key_ix = lax.broadcasted_iota(jnp.int32, (bs, tq), 0)
    qry_ix = lax.broadcasted_iota(jnp.int32, (bs, tq), 1)
    qry_row = lax.broadcasted_iota(jnp.int32, (1, tq), 1)

    def block_of(trip, g):
        j = trip * group + g - span
        n_past = jnp.clip(j, 0, nb - 1)
        in_past = j < i * span
        if g < span:
            is_own = trip == 0
            return jnp.where(is_own, i * span + g, n_past), is_own, jnp.logical_or(is_own, in_past)
        return n_past, None, in_past

    def visible(n, g, is_own, valid):
        ch = jnp.logical_and(sel_ref[pl.ds(n, 1), :] > 0.0, valid)
        if is_own is None:
            return ch
        return jnp.logical_or(ch, jnp.logical_and(is_own, qry_row // bs == g))

    def score_stage(trip, t_buf):
        mxs = []
        for g in range(group):
            n, is_own, valid = block_of(trip, g)
            off = pl.multiple_of(n * bs, bs)
            t = jnp.dot(k_ref[pl.ds(off, bs), :], qtb, preferred_element_type=F32) * c
            if is_own is not None:
                future = jnp.logical_and(qry_ix // bs == g, key_ix > qry_ix - g * bs)
                t = jnp.where(jnp.logical_and(future, is_own), NEG_INF, t)
            t_buf[g] = t
            mxs.append(jnp.where(visible(n, g, is_own, valid), jnp.max(t, axis=0, keepdims=True), NEG_INF))
        return tuple(mxs)

    def soft_stage(trip, t_buf, state, mxs):
        m, l, acc = state
        m_new = m
        for mx in mxs:
            m_new = jnp.maximum(m_new, mx)
        alpha = jnp.exp2(m - m_new)
        l = alpha * l
        acc = alpha * acc
        for g in range(group):
            n, is_own, valid = block_of(trip, g)
            off = pl.multiple_of(n * bs, bs)
            p = jnp.exp2(t_buf[g] - jnp.where(visible(n, g, is_own, valid), m_new, POS_INF))
            l = l + jnp.sum(p, axis=0, keepdims=True)
            acc = acc + jnp.dot(vt_ref[:, pl.ds(off, bs)], p.astype(BF16), preferred_element_type=F32)
        return m_new, l, acc

    def body(u, carry):
        state, mxs_a = carry
        mxs_b = score_stage(2 * u + 1, tb_ref)
        state = soft_stage(2 * u, ta_ref, state, mxs_a)
        mxs_a = score_stage(2 * u + 2, ta_ref)
        state = soft_stage(2 * u + 1, tb_ref, state, mxs_b)
        return state, mxs_a

    hd = qtb.shape[0]
    init = (jnp.full((1, tq), NEG_INF, F32), jnp.zeros((1, tq), F32), jnp.zeros((hd, tq), F32))
    n_trips = (i * span + span + group - 1) // group
    n_pairs = n_trips // 2
    state, mxs_a = lax.fori_loop(0, n_pairs, body, (init, score_stage(0, ta_ref)))
    m, l, acc = lax.cond(n_trips % 2 == 1, lambda: soft_stage(2 * n_pairs, ta_ref, state, mxs_a), lambda: state)
    o_ref[...] = (acc / l).T.astype(BF16)


def _gate_kernel(q_ref, km_ref, o_ref, *, nb):
    j = pl.program_id(1)
    tq = q_ref.shape[0]
    gate = lax.dot_general(km_ref[...], q_ref[...].astype(F32), (((1,), (1,)), ((), ())),
                           precision=lax.Precision.HIGHEST, preferred_element_type=F32)
    own = (j * tq + lax.broadcasted_iota(jnp.int32, gate.shape, 1)) // MOBA_BLOCK
    sel, _ = _top_blocks(gate, own, nb, axis=0)
    o_ref[0] = sel


def _moba_select(q, kmean):
    l, w = q.shape
    hd = w // N_HEADS
    nb = l // MOBA_BLOCK
    tq = _tile(l, 2048)
    return pl.pallas_call(
        functools.partial(_gate_kernel, nb=nb),
        grid=(N_HEADS, l // tq),
        in_specs=[pl.BlockSpec((tq, hd), lambda h, j: (j, h)),
                  pl.BlockSpec((nb, hd), lambda h, j: (0, h))],
        out_specs=pl.BlockSpec((1, nb, tq), lambda h, j: (h, 0, j)),
        out_shape=jax.ShapeDtypeStruct((N_HEADS, nb, l), F32),
        compiler_params=_cparams("parallel", "parallel"),
        name="moba_select",
    )(q, kmean)


def _moba_prompt(q, k, v, kmean, cache, page_table):
    l, w = q.shape
    hd = w // N_HEADS
    nb = l // MOBA_BLOCK
    n_phys, page, h_c, hd_c = cache.shape
    b, n_pages = page_table.shape
    ppb = MOBA_BLOCK // page
    tq = MOBA_BLOCK * Q_SPAN if nb % Q_SPAN == 0 else MOBA_BLOCK
    nq = l // tq
    steps = N_HEADS * nq
    total_pages = b * n_pages
    pps = ppb * pl.cdiv(pl.cdiv(total_pages, steps), ppb)
    bps = pps // ppb
    nb_past = n_pages // ppb
    assert MOBA_BLOCK == ppb * page and n_pages % pps == 0
    n_page_steps = total_pages // pps
    assert n_page_steps <= steps
    groups_per_seq = nb_past // bps
    sel = _moba_select(q, kmean)

    def page_spec(part):
        def imap(h, i, pt):
            s = jnp.minimum(h * nq + i, n_page_steps - 1)
            return (pt[s * pps + part], 0, 0, 0)
        return pl.BlockSpec((1, page, h_c, hd_c), imap)

    def mean_map(h, i, pt):
        s = jnp.minimum(h * nq + i, n_page_steps - 1)
        return (s // groups_per_seq, s % groups_per_seq, 0, 0)

    return pl.pallas_call(
        functools.partial(_moba_kernel, nb=nb, c=hd ** -0.5 * LOG2E, group=KV_GROUP, n_page_refs=pps, ppb=ppb,
                          n_page_steps=n_page_steps),
        grid_spec=pltpu.PrefetchScalarGridSpec(
            num_scalar_prefetch=1, grid=(N_HEADS, nq),
            in_specs=[pl.BlockSpec((tq, hd), lambda h, i, pt: (i, h)),
                      pl.BlockSpec((l, hd), lambda h, i, pt: (0, h)),
                      pl.BlockSpec((l, hd), lambda h, i, pt: (0, h)),
                      pl.BlockSpec((None, nb, tq), lambda h, i, pt: (h, 0, i))]
                     + [page_spec(part) for part in range(pps)],
            out_specs=[pl.BlockSpec((tq, hd), lambda h, i, pt: (i, h)),
                       pl.BlockSpec((1, bps, h_c, hd_c), mean_map)],
            scratch_shapes=[pltpu.VMEM((hd, l), BF16),
                            pltpu.VMEM((KV_GROUP, MOBA_BLOCK, tq), F32),
                            pltpu.VMEM((KV_GROUP, MOBA_BLOCK, tq), F32)]),
        out_shape=[jax.ShapeDtypeStruct((l, w), BF16), jax.ShapeDtypeStruct((b, nb_past, h_c, hd_c), F32)],
        compiler_params=_cparams("arbitrary", "arbitrary"),
        name="moba_prompt",
    )(page_table.reshape(-1), q, k, v, sel, *([cache] * pps))


def _mix_kernel(att_ref, u_ref, halo_ref, ga_ref, gb_ref, wa_ref, wp_ref, pm_ref, ps_ref, o_ref,
                ext_ref, mixed_ref, *, pos0, zero_first_halo):
    i = pl.program_id(0)
    j = pl.program_id(1)
    tm = u_ref.shape[0]
    pw = u_ref.shape[1]
    gw = pw // len(POOL_WINDOWS)

    @pl.when(j == 0)
    def _():
        halo = halo_ref[...]
        if zero_first_halo:
            halo = jnp.where(i == 0, 0.0, halo)
        ext_ref[0:POOL_HALO, :] = halo
        ext_ref[POOL_HALO:, :] = u_ref[...]
        pos = pos0 + i * tm + lax.broadcasted_iota(jnp.int32, (tm, 1), 0)
        for g, wdw in enumerate(POOL_WINDOWS):
            cols = slice(g * gw, (g + 1) * gw)
            tot = ext_ref[POOL_HALO:, cols]
            for d in range(1, wdw):
                tot = tot + ext_ref[POOL_HALO - d:POOL_HALO - d + tm, cols]
            cnt = jnp.minimum(wdw, pos + 1).astype(F32)
            pooled = tot / cnt - ext_ref[POOL_HALO:, cols]
            mixed = jnp.dot(pooled.astype(BF16), pm_ref[g], preferred_element_type=F32)
            mixed_ref[:, cols] = (mixed * ps_ref[:, cols]).astype(BF16)

    a = jnp.dot(att_ref[...], wa_ref[...], preferred_element_type=F32)
    b = jnp.dot(mixed_ref[...], wp_ref[...], preferred_element_type=F32)
    o_ref[...] = (jax.nn.sigmoid(ga_ref[...]) * a + jax.nn.sigmoid(gb_ref[...]) * b).astype(BF16)


def _branch_mix(att, zc, halo_src, wa, wp, pool_map, pool_scale, *, pos0, halo_is_state):
    m, d = att.shape
    pw = wp.shape[0]
    tn = _tile(d, 1024)
    nu = pw // tn if pw >= tn else None
    assert pw % tn == 0
    if halo_is_state:
        return _branch_mix_rows(att, zc, halo_src, wa, wp, pool_map, pool_scale)
    tm = _tile(m, 512)
    hb = tm // POOL_HALO
    return pl.pallas_call(
        functools.partial(_mix_kernel, pos0=pos0, zero_first_halo=True),
        grid=(m // tm, d // tn),
        in_specs=[pl.BlockSpec((tm, d), lambda i, j: (i, 0)),
                  pl.BlockSpec((tm, pw), lambda i, j: (i, 0)),
                  pl.BlockSpec((POOL_HALO, pw), lambda i, j: (jnp.maximum(i * hb - 1, 0), 0)),
                  pl.BlockSpec((tm, tn), lambda i, j: (i, nu + j)),
                  pl.BlockSpec((tm, tn), lambda i, j: (i, nu + d // tn + j)),
                  pl.BlockSpec((d, tn), lambda i, j: (0, j)),
                  pl.BlockSpec((pw, tn), lambda i, j: (0, j)),
                  pl.BlockSpec(pool_map.shape, lambda i, j: (0, 0, 0)),
                  pl.BlockSpec((1, pw), lambda i, j: (0, 0))],
        out_specs=pl.BlockSpec((tm, tn), lambda i, j: (i, j)),
        out_shape=jax.ShapeDtypeStruct((m, d), BF16),
        scratch_shapes=[pltpu.VMEM((tm + POOL_HALO, pw), F32), pltpu.VMEM((tm, pw), BF16)],
        compiler_params=_cparams("parallel", "arbitrary"),
        name="branch_mix",
    )(att, zc, zc, zc, zc, wa, wp, pool_map, pool_scale.reshape(1, pw))


def _mix_rows_kernel(att_ref, zc_ref, st_ref, wa_ref, wp_ref, pm_ref, ps_ref, o_ref, mixed_ref):
    nbuf, _, pw = st_ref.shape
    d = att_ref.shape[1]
    gw = pw // len(POOL_WINDOWS)
    u = zc_ref[:, 0:pw]
    for g, wdw in enumerate(POOL_WINDOWS):
        cols = slice(g * gw, (g + 1) * gw)
        tot = u[:, cols]
        for r in range(nbuf - (wdw - 1), nbuf):
            tot = tot + st_ref[r, :, cols]
        pooled = tot / float(wdw) - u[:, cols]
        mixed = jnp.dot(pooled.astype(BF16), pm_ref[g], preferred_element_type=F32)
        mixed_ref[:, cols] = (mixed * ps_ref[:, cols]).astype(BF16)
    a = jnp.dot(att_ref[...], wa_ref[...], preferred_element_type=F32)
    b = jnp.dot(mixed_ref[...], wp_ref[...], preferred_element_type=F32)
    ga = zc_ref[:, pw:pw + d]
    gb = zc_ref[:, pw + d:pw + 2 * d]
    o_ref[...] = (jax.nn.sigmoid(ga) * a + jax.nn.sigmoid(gb) * b).astype(BF16)


def _branch_mix_rows(att, zc, state, wa, wp, pool_map, pool_scale):
    m, d = att.shape
    pw = wp.shape[0]
    full = lambda a: pl.BlockSpec(a.shape, lambda i: (0,) * a.ndim)
    ps = pool_scale.reshape(1, pw)
    args = (att, zc, state, wa, wp, pool_map, ps)
    return pl.pallas_call(
        _mix_rows_kernel, grid=(1,), in_specs=[full(a) for a in args],
        out_specs=pl.BlockSpec((m, d), lambda i: (0, 0)),
        out_shape=jax.ShapeDtypeStruct((m, d), BF16),
        scratch_shapes=[pltpu.VMEM((m, pw), BF16)],
        compiler_params=_cparams("arbitrary"),
        name="branch_mix_rows",
    )(*args)


def _outproj_kernel(m_ref, w_ref, x_ref, o_ref):
    o_ref[...] = x_ref[...] + jnp.dot(m_ref[...], w_ref[...], preferred_element_type=F32)


def _out_project(mix, w, x):
    m, d = x.shape
    tm = _tile(m, 1024)
    tn = _tile(d, 1024)
    return pl.pallas_call(
        _outproj_kernel, grid=(m // tm, d // tn),
        in_specs=[pl.BlockSpec((tm, d), lambda i, j: (i, 0)),
                  pl.BlockSpec((d, tn), lambda i, j: (0, j)),
                  pl.BlockSpec((tm, tn), lambda i, j: (i, j))],
        out_specs=pl.BlockSpec((tm, tn), lambda i, j: (i, j)),
        out_shape=jax.ShapeDtypeStruct((m, d), F32),
        compiler_params=_cparams("parallel", "parallel"),
        name="out_project",
    )(mix, w, x)


def _sample_gate_kernel(q_ref, km_ref, o_ref, *, nb):
    gate = jnp.sum(km_ref[0] * q_ref[...], axis=-1, keepdims=True)
    _, picks = _top_blocks(gate, nb, nb, axis=0)
    for t, first in enumerate(picks):
        o_ref[0, t] = first[0]


def _sample_gate(q, kmean):
    b, nb, h, hd = kmean.shape
    n_sel = min(MOBA_TOPK, nb)
    return pl.pallas_call(
        functools.partial(_sample_gate_kernel, nb=nb),
        grid=(b,),
        in_specs=[pl.BlockSpec((1, h, hd), lambda i: (i, 0, 0)),
                  pl.BlockSpec((1, nb, h, hd), lambda i: (i, 0, 0, 0))],
        out_specs=pl.BlockSpec((1, n_sel, h, 1), lambda i: (i, 0, 0, 0)),
        out_shape=jax.ShapeDtypeStruct((b, n_sel, h, 1), jnp.int32),
        compiler_params=_cparams("parallel"),
        name="sample_gate",
    )(q, kmean)


def _paged_attn_kernel(sel_ref, pt_ref, q_ref, kn_ref, vn_ref, ck_ref, cv_ref, o_ref, kbuf, vbuf, sems,
                       *, scale, n_sel, n_pages, ppb):
    b = pl.program_id(0)
    nbatch = pl.num_programs(0)
    h_all, page = ck_ref.shape[2], ck_ref.shape[1]

    def copies(bi, slot):
        out = []
        for h in range(h_all):
            for t in range(n_sel):
                blk = sel_ref[(bi * n_sel + t) * h_all + h]
                for part in range(ppb):
                    pg = pt_ref[bi * n_pages + blk * ppb + part]
                    rows = pl.ds((t * ppb + part) * page, page)
                    out.append(pltpu.make_async_copy(ck_ref.at[pg, :, h, :], kbuf.at[slot, h, rows, :],
                                                     sems.at[slot, 0]))
                    out.append(pltpu.make_async_copy(cv_ref.at[pg, :, h, :], vbuf.at[slot, h, rows, :],
                                                     sems.at[slot, 1]))
        return out

    slot = b % 2

    @pl.when(b == 0)
    def _():
        for c in copies(b, slot):
            c.start()

    @pl.when(b + 1 < nbatch)
    def _():
        for c in copies(b + 1, 1 - slot):
            c.start()

    for c in copies(b, slot):
        c.wait()

    for h in range(h_all):
        q = q_ref[0, h:h + 1, :]
        kn = kn_ref[0, h:h + 1, :]
        vn = vn_ref[0, h:h + 1, :]
        qs = q * scale
        s = jnp.sum(kbuf[slot, h] * qs, axis=1, keepdims=True)
        s0 = jnp.sum(qs * kn, axis=1, keepdims=True)
        m = jnp.maximum(jnp.max(s, axis=0, keepdims=True), s0)
        pr = jnp.exp(s - m)
        p0 = jnp.exp(s0 - m)
        l = jnp.sum(pr, axis=0, keepdims=True) + p0
        pv = jnp.sum(pr * vbuf[slot, h], axis=0, keepdims=True)
        o_ref[0, h:h + 1, :] = (pv + p0 * vn) / l


def _paged_attention(q, k_new, v_new, cache_k, cache_v, sel, page_table):
    b, h, hd = q.shape
    n_sel = sel.shape[1]
    n_pages = page_table.shape[1]
    page = cache_k.shape[1]
    ppb = MOBA_BLOCK // page
    tok = pl.BlockSpec((1, h, hd), lambda bi, s, t: (bi, 0, 0))
    hbm = pl.BlockSpec(memory_space=pl.ANY)
    return pl.pallas_call(
        functools.partial(_paged_attn_kernel, scale=hd ** -0.5, n_sel=n_sel, n_pages=n_pages, ppb=ppb),
        grid_spec=pltpu.PrefetchScalarGridSpec(
            num_scalar_prefetch=2, grid=(b,),
            in_specs=[tok, tok, tok, hbm, hbm],
            out_specs=tok,
            scratch_shapes=[pltpu.VMEM((2, h, n_sel * MOBA_BLOCK, hd), F32),
                            pltpu.VMEM((2, h, n_sel * MOBA_BLOCK, hd), F32),
                            pltpu.SemaphoreType.DMA((2, 2))]),
        out_shape=jax.ShapeDtypeStruct((b, h, hd), F32),
        compiler_params=_cparams("arbitrary"),
        name="paged_attention",
    )(sel.reshape(-1), page_table.reshape(-1), q, k_new, v_new, cache_k, cache_v)


def _layer(x, pos, weights, attend, *, pos0, pool_prefix):
    (g1, f1g, f1u, f1d, gm, w_in, pool_map, pool_scale, wa, wp, wo, g2, f2g, f2u, f2d, gf) = weights
    m, d = x.shape
    hd = d // N_HEADS
    x1, h = _half_ffn(x, g1, f1g, f1u, f1d, gm, final=False)
    tabs = _rotary_tables(pos, hd)
    (q,) = _project(h, w_in, 0, d, "q", tabs)
    want_kmean = pool_prefix is None
    kres = _project(h, w_in, d, d, "k", tabs, want_kmean=want_kmean)
    k, kb = kres[0], kres[1]
    kmean = kres[2].reshape(-1, d) if want_kmean else None
    v, vb = _project(h, w_in, 2 * d, d, "v")
    (zc,) = _project(h, w_in, 3 * d, w_in.shape[1] - 3 * d, "plain")
    att = attend(q, kb, vb, kmean)
    mix = _branch_mix(att, zc, zc if pool_prefix is None else pool_prefix, wa, wp, pool_map, pool_scale,
                      pos0=pos0, halo_is_state=pool_prefix is not None)
    x2 = _out_project(mix, wo, x1)
    y = _half_ffn(x2, g2, f2g, f2u, f2d, gf, final=True)
    return y, k, v, zc[:, :wp.shape[0]]


def kernel(x_prompt, x_sample, cache_k, cache_v, page_table, state_pool, norm_ffn1, ffn1_gate, ffn1_up, ffn1_down,
           norm_mix, w_in, pool_map, pool_scale, w_branch_attn, w_branch_pool, w_out, norm_ffn2, ffn2_gate,
           ffn2_up, ffn2_down, norm_final):
    batch, seq, d = x_prompt.shape
    dec_batch, dec_seq, _ = x_sample.shape
    depth, n_phys, page, _, hd = cache_k.shape
    n_pages = page_table.shape[1]
    past_len = n_pages * page
    pw = pool_scale.shape[1]
    nbuf = state_pool.shape[2]
    assert batch == 1 and dec_seq == 1 and depth == 1 and d == N_HEADS * hd
    assert seq % MOBA_BLOCK == 0 and past_len % MOBA_BLOCK == 0 and nbuf == max(POOL_WINDOWS) - 1
    assert past_len + 1 >= max(POOL_WINDOWS)

    lyr = 0
    wb = _to_bf16
    weights = (norm_ffn1[lyr], wb(ffn1_gate[lyr]), wb(ffn1_up[lyr]), wb(ffn1_down[lyr]), norm_mix[lyr],
               wb(w_in[lyr]), pool_map[lyr].astype(BF16), pool_scale[lyr], wb(w_branch_attn[lyr]),
               wb(w_branch_pool[lyr]), wb(w_out[lyr]),
               norm_ffn2[lyr], wb(ffn2_gate[lyr]), wb(ffn2_up[lyr]), wb(ffn2_down[lyr]), norm_final)

    ck, cv = cache_k[lyr], cache_v[lyr]
    page_means = []

    def attend_prompt(q, kb, vb, kmean):
        att, means = _moba_prompt(q, kb, vb, kmean, ck, page_table)
        page_means.append(means)
        return att

    pos_p = jnp.arange(seq, dtype=jnp.int32)
    y_p, k_p, v_p, u_p = _layer(x_prompt[0], pos_p, weights, attend_prompt, pos0=0, pool_prefix=None)

    def attend_sample(q, kb, vb, _):
        tok = lambda a: a.astype(F32).reshape(dec_batch, N_HEADS, hd)
        sel = _sample_gate(tok(q), page_means[0])
        att = _paged_attention(tok(q), tok(kb), tok(vb), ck, cv, sel[..., 0], page_table)
        return att.reshape(dec_batch, d).astype(BF16)

    pos_s = jnp.full((dec_batch,), past_len, dtype=jnp.int32)
    y_s, k_s, v_s, u_s = _layer(x_sample[:, 0], pos_s, weights, attend_sample, pos0=past_len,
                                pool_prefix=jnp.swapaxes(state_pool[lyr], 0, 1))

    heads = lambda a, b_, l_: a.reshape(1, b_, l_, N_HEADS, hd)
    pool_p = u_p[seq - nbuf:].reshape(1, 1, nbuf, pw)
    pool_s = jnp.concatenate([state_pool[lyr][:, 1:], u_s[:, None, :]], axis=1)[None]
    return (y_p[None], y_s[:, None], heads(k_p, 1, seq), heads(v_p, 1, seq),
            heads(k_s, dec_batch, 1), heads(v_s, dec_batch, 1), pool_p, pool_s)
```
